```python
import jax, jax.numpy as jnp
from jax import lax
import numpy as np

D_MODEL = 1024
BATCH = 2
SEQ = 16384
DEPTH = 4
DEC_BATCH = 16
DEC_SEQ = 64
PAST_LEN = 1024

CHUNK = 64
N_MIXERS = 3
POOL_WINDOWS = (2, 4, 8, 16)
N_POOL_GROUPS = 4
POOL_GROUP = D_MODEL // N_POOL_GROUPS
POOL_STATE = max(POOL_WINDOWS) - 1
FOX_HEADS = 16
FOX_HEAD_DIM = D_MODEL // FOX_HEADS
Q_BLOCK = 128
HGRN_EXPAND = 128
HGRN_HEADS = D_MODEL // HGRN_EXPAND
HGRN_DK = HGRN_EXPAND
HGRN_DV = D_MODEL // HGRN_HEADS
D_FF = 4 * D_MODEL
ALPHA = (2 * DEPTH) ** 0.25
BETA = (8 * DEPTH) ** -0.25
LN_EPS = 1e-5
RMS_EPS = 1e-6
N_POOL = (DEPTH + 2) // 3
N_FOX = (DEPTH + 1) // 3
N_HGRN = DEPTH // 3
F32 = jnp.float32

kernel_name = 'hybrid_stream_pool_fox_hgrn2_step'


def layer_norm(x, g, b):
    xf = x.astype(F32)
    mu = jnp.mean(xf, -1, keepdims=True)
    var = jnp.mean(jnp.square(xf - mu), -1, keepdims=True)
    y = (xf - mu) * lax.rsqrt(var + LN_EPS)
    return (y * g.astype(F32) + b.astype(F32)).astype(x.dtype)


def pool_mixer(x, hist, pos0, w_grp, scale):
    B, L, D = x.shape
    xa = jnp.concatenate([hist.astype(x.dtype), x], axis=1)
    cs = jnp.cumsum(xa.astype(F32), axis=1)
    cs = jnp.concatenate([jnp.zeros((B, 1, D), F32), cs], axis=1)
    n_avail = jnp.arange(L) + pos0 + 1
    parts = []
    for g, w in enumerate(POOL_WINDOWS):
        sl = slice(g * POOL_GROUP, (g + 1) * POOL_GROUP)
        hi = cs[:, POOL_STATE + 1:POOL_STATE + 1 + L, sl]
        lo = cs[:, POOL_STATE + 1 - w:POOL_STATE + 1 - w + L, sl]
        cnt = jnp.minimum(n_avail, w).astype(F32)[None, :, None]
        parts.append((hi - lo) / cnt)
    pooled = jnp.concatenate(parts, -1) - x.astype(F32)
    pooled = pooled.astype(x.dtype).reshape(B, L, N_POOL_GROUPS, POOL_GROUP)
    y = jnp.einsum('blgc,gcd->blgd', pooled, w_grp).reshape(B, L, D) * scale
    return y, xa[:, -POOL_STATE:]


def fox_project(x, w_in, b_f):
    B, L, D = x.shape
    h = x @ w_in
    q = h[..., :D].reshape(B, L, FOX_HEADS, FOX_HEAD_DIM)
    k = h[..., D:2 * D].reshape(B, L, FOX_HEADS, FOX_HEAD_DIM)
    v = h[..., 2 * D:3 * D].reshape(B, L, FOX_HEADS, FOX_HEAD_DIM)
    logf = jax.nn.log_sigmoid((h[..., 3 * D:] + b_f).astype(F32))
    return q, k, v, logf


def fox_attend(q, cq, qpos, k, v, ck):
    s = jnp.einsum('bqhd,bkhd->bhqk', q, k, preferred_element_type=F32) * (FOX_HEAD_DIM ** -0.5)
    bias = jnp.transpose(cq, (0, 2, 1))[..., :, None] - jnp.transpose(ck, (0, 2, 1))[..., None, :]
    mask = jnp.arange(k.shape[1])[None, :] <= qpos[:, None]
    p = jax.nn.softmax(jnp.where(mask, s + bias, -jnp.inf), axis=-1)
    return jnp.einsum('bhqk,bkhd->bqhd', p.astype(v.dtype), v)


def fox_prompt(x, w_in, b_f, w_o):
    B, L, D = x.shape
    q, k, v, logf = fox_project(x, w_in, b_f)
    c = jnp.cumsum(logf, axis=1)
    nb = L // Q_BLOCK
    qb = q.reshape(B, nb, Q_BLOCK, FOX_HEADS, FOX_HEAD_DIM).transpose(1, 0, 2, 3, 4)
    cb = c.reshape(B, nb, Q_BLOCK, FOX_HEADS).transpose(1, 0, 2, 3)
    pos = jnp.arange(L).reshape(nb, Q_BLOCK)
    o = lax.map(lambda a: fox_attend(a[0], a[1], a[2], k, v, c), (qb, cb, pos))
    o = o.transpose(1, 0, 2, 3, 4).reshape(B, L, D)
    return o @ w_o, k, v, logf


def fox_sample(x, k_cache, v_cache, lf_cache, w_in, b_f, w_o):
    B, L, D = x.shape
    P = k_cache.shape[1]
    q, k, v, logf = fox_project(x, w_in, b_f)
    k_all = jnp.concatenate([k_cache.astype(k.dtype), k], axis=1)
    v_all = jnp.concatenate([v_cache.astype(v.dtype), v], axis=1)
    c_all = jnp.cumsum(jnp.concatenate([lf_cache.astype(F32), logf], axis=1), axis=1)
    o = fox_attend(q, c_all[:, P:], P + jnp.arange(L), k_all, v_all, c_all)
    return o.reshape(B, L, D) @ w_o, k, v, logf


def hgrn2_mixer(x, S0, w_in, lb, norm_g, w_o):
    B, L, D = x.shape
    q, fr, v, g = jnp.split(x @ w_in, 4, axis=-1)
    q = jax.nn.silu(q.astype(F32))
    fr = fr.astype(F32)
    logf = jnp.log(lb + (1.0 - lb) * jax.nn.sigmoid(fr))
    k = (1.0 - lb) * jax.nn.sigmoid(-fr)
    C = min(CHUNK, L)
    nc = L // C

    def to_chunks(t):
        return t.reshape(B, nc, C, HGRN_HEADS, -1).transpose(1, 0, 3, 2, 4)

    tri = jnp.tril(jnp.ones((C, C), bool))[None, None, :, :, None]

    def step(S, inp):
        qc, kc, vc, lfc = inp
        b = jnp.cumsum(lfc, axis=2)
        o_inter = jnp.einsum('bhtk,bhkv->bhtv', qc * jnp.exp(b), S)
        dec = jnp.exp(jnp.where(tri, b[:, :, :, None, :] - b[:, :, None, :, :], -jnp.inf))
        A = jnp.einsum('bhtk,bhsk,bhtsk->bhts', qc, kc, dec)
        o_intra = jnp.einsum('bhts,bhsv->bhtv', A, vc)
        b_last = b[:, :, -1:, :]
        S_new = jnp.exp(b[:, :, -1, :])[..., None] * S + jnp.einsum('bhsk,bhsv->bhkv', kc * jnp.exp(b_last - b), vc)
        return S_new, o_inter + o_intra

    S_fin, o = lax.scan(step, S0.astype(F32),
                        (to_chunks(q), to_chunks(k), to_chunks(v.astype(F32)), to_chunks(logf)))
    o = o.transpose(1, 0, 3, 2, 4).reshape(B, L, HGRN_HEADS, HGRN_DV)
    o = o * lax.rsqrt(jnp.mean(o * o, -1, keepdims=True) + RMS_EPS)
    o = o.reshape(B, L, D) * norm_g.astype(F32) * jax.nn.silu(g.astype(F32))
    return o.astype(x.dtype) @ w_o, S_fin


def sq_relu_mlp(x, w_up, w_down):
    return jnp.square(jax.nn.relu(x @ w_up)) @ w_down


def setup_inputs(seed: int = 0) -> dict:
    key = jax.random.key(seed)
    ks = jax.random.split(key, 24)
    D = D_MODEL
    nrm = jax.random.normal
    return {
        'x_prompt': nrm(ks[0], (BATCH, SEQ, D), F32),
        'x_sample': nrm(ks[1], (DEC_BATCH, DEC_SEQ, D), F32),
        'cache_fox_k': nrm(ks[2], (N_FOX, DEC_BATCH, PAST_LEN, FOX_HEADS, FOX_HEAD_DIM), F32),
        'cache_fox_v': nrm(ks[3], (N_FOX, DEC_BATCH, PAST_LEN, FOX_HEADS, FOX_HEAD_DIM), F32),
        'cache_fox_logf': jax.nn.log_sigmoid(2.0 + 0.5 * nrm(ks[4], (N_FOX, DEC_BATCH, PAST_LEN, FOX_HEADS), F32)),
        'state_pool': nrm(ks[5], (N_POOL, DEC_BATCH, POOL_STATE, D), F32),
        'state_hgrn': 0.5 * nrm(ks[6], (N_HGRN, DEC_BATCH, HGRN_HEADS, HGRN_DK, HGRN_DV), F32),
        'pool_w': nrm(ks[7], (N_POOL, N_POOL_GROUPS, POOL_GROUP, POOL_GROUP), F32) * (POOL_GROUP ** -0.5) * BETA,
        'pool_scale': 1.0 + 0.02 * nrm(ks[8], (N_POOL, D), F32),
        'fox_w_in': nrm(ks[9], (N_FOX, D, 3 * D + FOX_HEADS), F32) * (D ** -0.5),
        'fox_b_f': 2.0 + 0.5 * nrm(ks[10], (N_FOX, FOX_HEADS), F32),
        'fox_w_o': nrm(ks[11], (N_FOX, D, D), F32) * (D ** -0.5) * BETA,
        'hgrn_w_in': nrm(ks[12], (N_HGRN, D, 4 * D), F32) * (D ** -0.5),
        'hgrn_lb_raw': 0.1 * nrm(ks[13], (DEPTH, D), F32),
        'hgrn_norm_g': 1.0 + 0.02 * nrm(ks[14], (N_HGRN, D), F32),
        'hgrn_w_o': nrm(ks[15], (N_HGRN, D, D), F32) * (D ** -0.5) * BETA,
        'ln_mix_g': 1.0 + 0.02 * nrm(ks[16], (DEPTH, D), F32),
        'ln_mix_b': 0.02 * nrm(ks[17], (DEPTH, D), F32),
        'w_up': nrm(ks[18], (DEPTH, D, D_FF), F32) * (D ** -0.5),
        'w_down': nrm(ks[19], (DEPTH, D_FF, D), F32) * (D_FF ** -0.5) * BETA,
        'ln_ffn_g': 1.0 + 0.02 * nrm(ks[20], (DEPTH, D), F32),
        'ln_ffn_b': 0.02 * nrm(ks[21], (DEPTH, D), F32),
    }


def reference(x_prompt, x_sample, cache_fox_k, cache_fox_v, cache_fox_logf, state_pool, state_hgrn,
              pool_w, pool_scale, fox_w_in, fox_b_f, fox_w_o, hgrn_w_in, hgrn_lb_raw, hgrn_norm_g, hgrn_w_o,
              ln_mix_g, ln_mix_b, w_up, w_down, ln_ffn_g, ln_ffn_b):
    lb_p = jax.nn.softmax(hgrn_lb_raw.astype(F32), axis=0)
    lb_all = jnp.cumsum(lb_p, axis=0) - lb_p[0]
    xp, xs = x_prompt, x_sample
    pool_p, pool_s = [], []
    fk_p, fv_p, fl_p, fk_s, fv_s, fl_s = [], [], [], [], [], []
    hg_p, hg_s = [], []
    for i in range(DEPTH):
        kind = i % N_MIXERS
        j = i // N_MIXERS
        if kind == 0:
            hist0 = jnp.zeros((xp.shape[0], POOL_STATE, D_MODEL), xp.dtype)
            mp, st_p = pool_mixer(xp, hist0, 0, pool_w[j], pool_scale[j])
            ms, st_s = pool_mixer(xs, state_pool[j], PAST_LEN, pool_w[j], pool_scale[j])
            pool_p.append(st_p)
            pool_s.append(st_s)
        elif kind == 1:
            mp, kp_, vp_, lp_ = fox_prompt(xp, fox_w_in[j], fox_b_f[j], fox_w_o[j])
            ms, ks_, vs_, ls_ = fox_sample(xs, cache_fox_k[j], cache_fox_v[j], cache_fox_logf[j],
                                           fox_w_in[j], fox_b_f[j], fox_w_o[j])
            fk_p.append(kp_)
            fv_p.append(vp_)
            fl_p.append(lp_)
            fk_s.append(ks_)
            fv_s.append(vs_)
            fl_s.append(ls_)
        else:
            S0 = jnp.zeros((xp.shape[0], HGRN_HEADS, HGRN_DK, HGRN_DV), F32)
            mp, sp_ = hgrn2_mixer(xp, S0, hgrn_w_in[j], lb_all[i], hgrn_norm_g[j], hgrn_w_o[j])
            ms, ss_ = hgrn2_mixer(xs, state_hgrn[j], hgrn_w_in[j], lb_all[i], hgrn_norm_g[j], hgrn_w_o[j])
            hg_p.append(sp_)
            hg_s.append(ss_)
        xp = layer_norm(ALPHA * xp + mp, ln_mix_g[i], ln_mix_b[i])
        xs = layer_norm(ALPHA * xs + ms, ln_mix_g[i], ln_mix_b[i])
        xp = layer_norm(ALPHA * xp + sq_relu_mlp(xp, w_up[i], w_down[i]), ln_ffn_g[i], ln_ffn_b[i])
        xs = layer_norm(ALPHA * xs + sq_relu_mlp(xs, w_up[i], w_down[i]), ln_ffn_g[i], ln_ffn_b[i])
    return (xp, xs, jnp.stack(pool_p), jnp.stack(pool_s),
            jnp.stack(fk_p), jnp.stack(fv_p), jnp.stack(fl_p),
            jnp.stack(fk_s), jnp.stack(fv_s), jnp.stack(fl_s),
            jnp.stack(hg_p), jnp.stack(hg_s))
```

```python
import functools

import jax
import jax.numpy as jnp
import numpy as np
from jax import lax
from jax.experimental import pallas as pl
from jax.experimental.pallas import tpu as pltpu

F32 = jnp.float32
BF16 = jnp.bfloat16

CHUNK = 64
N_MIXERS = 3
POOL_WINDOWS = (2, 4, 8, 16)
POOL_HIST = 16
FOX_HEADS = 16
HGRN_HEAD_DIM = 128
LN_EPS = 1e-5
RMS_EPS = 1e-6

LANES = 128
SUBLANES = 8
VMEM_LIMIT_BYTES = 56 * 1024 * 1024

HEAD_SLOT = LANES
MASK_VALUE = -1e30


def _compiler_params(semantics):
    return pltpu.CompilerParams(dimension_semantics=semantics, vmem_limit_bytes=VMEM_LIMIT_BYTES)


def _resident(shape):
    return pl.BlockSpec(shape, lambda *_: (0,) * len(shape), pipeline_mode=pl.Buffered(1))


def _pick_tile(n, pref):
    t = min(n, pref)
    while n % t:
        t //= 2
    return t


def _layer_norm(z, g, b):
    mu = jnp.mean(z, -1, keepdims=True)
    zc = z - mu
    var = jnp.mean(zc * zc, -1, keepdims=True)
    return zc * lax.rsqrt(var + LN_EPS) * g + b


def _split3(x):
    hi = x.astype(BF16)
    r1 = x - hi.astype(F32)
    mid = r1.astype(BF16)
    lo = (r1 - mid.astype(F32)).astype(BF16)
    return hi, mid, lo


def _dot(a, b):
    return jnp.dot(a, b, preferred_element_type=F32)


def _dot_nt(a, b):
    return lax.dot_general(a, b, (((1,), (1,)), ((), ())), preferred_element_type=F32)


def _dot_tn(a, b):
    return lax.dot_general(a, b, (((0,), (0,)), ((), ())), preferred_element_type=F32)


def _dot_exact_lhs01(m01, x):
    hi, mid, lo = _split3(x)
    return _dot(m01, hi) + _dot(m01, mid) + _dot(m01, lo)


def _mlp_ln_body(x_ref, wup_ref, wdn_ref, g_ref, b_ref, o_ref, *, alpha, ff_chunk):
    x = x_ref[...]
    xb = x.astype(BF16)
    d_ff = wup_ref.shape[1]
    acc = jnp.zeros(x.shape, F32)
    for c in range(d_ff // ff_chunk):
        sl = slice(c * ff_chunk, (c + 1) * ff_chunk)
        h = jnp.maximum(_dot(xb, wup_ref[:, sl]), 0.0)
        acc = acc + _dot((h * h).astype(BF16), wdn_ref[sl, :])
    o_ref[...] = _layer_norm(alpha * x + acc, g_ref[...], b_ref[...])


def _mlp_ln(x2, wup, wdn, g, b, *, alpha):
    R, D = x2.shape
    d_ff = wup.shape[1]
    tm = _pick_tile(R, 512)
    return pl.pallas_call(
        functools.partial(_mlp_ln_body, alpha=alpha, ff_chunk=_pick_tile(d_ff, 512)),
        grid=(R // tm,),
        in_specs=[pl.BlockSpec((tm, D), lambda i: (i, 0)),
                  _resident((D, d_ff)), _resident((d_ff, D)),
                  _resident((1, D)), _resident((1, D))],
        out_specs=pl.BlockSpec((tm, D), lambda i: (i, 0)),
        out_shape=jax.ShapeDtypeStruct((R, D), F32),
        compiler_params=_compiler_params(("parallel",)),
        name="mlp_ln",
    )(x2, wup, wdn, g, b)


def _proj_ln_body(a_ref, w_ref, x_ref, g_ref, b_ref, o_ref, *, alpha):
    y = _dot(a_ref[...], w_ref[...])
    o_ref[...] = _layer_norm(alpha * x_ref[...] + y, g_ref[...], b_ref[...])


def _proj_ln(a2, w, x2, g, b, *, alpha):
    R, D = x2.shape
    tm = _pick_tile(R, 512)
    return pl.pallas_call(
        functools.partial(_proj_ln_body, alpha=alpha),
        grid=(R // tm,),
        in_specs=[pl.BlockSpec((tm, D), lambda i: (i, 0)),
                  _resident((D, D)),
                  pl.BlockSpec((tm, D), lambda i: (i, 0)),
                  _resident((1, D)), _resident((1, D))],
        out_specs=pl.BlockSpec((tm, D), lambda i: (i, 0)),
        out_shape=jax.ShapeDtypeStruct((R, D), F32),
        compiler_params=_compiler_params(("parallel",)),
        name="proj_ln",
    )(a2, w, x2, g, b)


def _pool_ln_body(x_ref, xprev_ref, hist_ref, w_ref, scale_ref, g_ref, b_ref, o_ref, xa_ref, *, alpha, pos0, tm):
    i = pl.program_id(1)
    x = x_ref[0]
    group = x.shape[1] // len(POOL_WINDOWS)
    xa_ref[0:POOL_HIST, :] = jnp.where(i == 0, hist_ref[0], xprev_ref[0])
    xa_ref[POOL_HIST:, :] = x
    n_avail = lax.broadcasted_iota(jnp.int32, (tm, 1), 0) + (i * tm + pos0 + 1)
    ys = []
    for gi, w in enumerate(POOL_WINDOWS):
        sl = slice(gi * group, (gi + 1) * group)
        s = xa_ref[:, sl]
        sh = 1
        while sh < w:
            s = s + pltpu.roll(s, sh, axis=0)
            sh *= 2
        cnt = jnp.minimum(n_avail, w).astype(F32)
        pooled = s[POOL_HIST:, :] / cnt - x[:, sl]
        ys.append(_dot(pooled.astype(BF16), w_ref[gi]))
    y = jnp.concatenate(ys, -1) * scale_ref[...]
    o_ref[0] = _layer_norm(alpha * x + y, g_ref[...], b_ref[...])


def _pool_ln(x, hist, w, scale, g, b, *, alpha, pos0):
    B, L, D = x.shape
    n_g, group = w.shape[0], w.shape[1]
    tm = _pick_tile(L, 512)
    assert tm % POOL_HIST == 0 and max(POOL_WINDOWS) <= POOL_HIST
    per = tm // POOL_HIST
    return pl.pallas_call(
        functools.partial(_pool_ln_body, alpha=alpha, pos0=pos0, tm=tm),
        grid=(B, L // tm),
        in_specs=[pl.BlockSpec((1, tm, D), lambda bi, i: (bi, i, 0)),
                  pl.BlockSpec((1, POOL_HIST, D), lambda bi, i: (bi, jnp.maximum(i * per - 1, 0), 0)),
                  pl.BlockSpec((1, POOL_HIST, D), lambda bi, i: (bi, 0, 0)),
                  _resident((n_g, group, group)),
                  _resident((1, D)), _resident((1, D)), _resident((1, D))],
        out_specs=pl.BlockSpec((1, tm, D), lambda bi, i: (bi, i, 0)),
        out_shape=jax.ShapeDtypeStruct((B, L, D), F32),
        scratch_shapes=[pltpu.VMEM((tm + POOL_HIST, D), F32)],
        compiler_params=_compiler_params(("parallel", "arbitrary")),
        name="pool_ln",
    )(x, x, hist, w, scale, g, b)


def _placement_matrices(heads, head_dim):
    assert 3 * heads < LANES and head_dim + 6 <= HEAD_SLOT
    pq = np.zeros((LANES, heads * HEAD_SLOT), np.float32)
    pk = np.zeros((LANES, heads * HEAD_SLOT), np.float32)
    one = 3 * heads
    for h in range(heads):
        base = h * HEAD_SLOT + head_dim
        for p in range(3):
            pq[p * heads + h, base + p] = 1.0
            pq[one, base + 3 + p] = 1.0
            pk[one, base + p] = 1.0
            pk[p * heads + h, base + 3 + p] = -1.0
    return jnp.asarray(pq, BF16), jnp.asarray(pk, BF16)


def _bias_parts(c, heads):
    hi, mid, lo = _split3(c)
    lane = lax.broadcasted_iota(jnp.int32, c.shape, 1)
    parts = (hi.astype(F32) + pltpu.roll(mid.astype(F32), heads, axis=1)
             + pltpu.roll(lo.astype(F32), 2 * heads, axis=1) + jnp.where(lane == 3 * heads, 1.0, 0.0))
    return parts.astype(BF16)


def _slotted(nat, aug, head_dim):
    T = nat.shape[0]
    per = LANES // head_dim
    assert per == 2
    lane = lax.broadcasted_iota(jnp.int32, (T, LANES), 1)
    low = lane < head_dim
    out = []
    for p in range(nat.shape[1] // LANES):
        pair = nat[:, p * LANES:(p + 1) * LANES]
        swapped = pltpu.roll(pair, head_dim, axis=1)
        out.append(jnp.where(low, pair, aug[:, (2 * p) * HEAD_SLOT:(2 * p + 1) * HEAD_SLOT]))
        out.append(jnp.where(low, swapped, aug[:, (2 * p + 1) * HEAD_SLOT:(2 * p + 2) * HEAD_SLOT]))
    return jnp.concatenate(out, -1).astype(BF16)


def _log_sigmoid(x):
    return jnp.minimum(x, 0.0) - jnp.log(1.0 + jnp.exp(-jnp.abs(x)))


def _fox_proj_body(x_ref, w_ref, bf_ref, c0_ref, tri_ref, pq_ref, pk_ref,
                   k_ref, v_ref, lf_ref, qs_ref, ks_ref, vb_ref, carry_ref, *, heads):
    l = pl.program_id(1)
    D = x_ref.shape[2]

    @pl.when(l == 0)
    def _():
        carry_ref[...] = c0_ref[0]

    h = _dot(x_ref[0].astype(BF16), w_ref[...])
    q, k, v = h[:, :D], h[:, D:2 * D], h[:, 2 * D:3 * D]
    fr = h[:, 3 * D:] + bf_ref[...]
    lane = lax.broadcasted_iota(jnp.int32, fr.shape, 1)
    logf = jnp.where(lane < heads, _log_sigmoid(fr), 0.0)
    c = _dot_exact_lhs01(tri_ref[...], logf) + carry_ref[...]
    carry_ref[...] = c[c.shape[0] - 1:, :]
    parts = _bias_parts(c, heads)
    hd = D // heads
    k_ref[0] = k
    v_ref[0] = v
    vb_ref[0] = v.astype(BF16)
    lf_ref[0] = logf[:, :heads]
    qs_ref[0] = _slotted(q, _dot(parts, pq_ref[...]), hd)
    ks_ref[0] = _slotted(k, _dot(parts, pk_ref[...]), hd)


def _fox_proj(x, w_cat, bf_pad, c0, *, heads):
    B, L, D = x.shape
    T = _pick_tile(L, 256)
    tri = jnp.asarray(np.tril(np.ones((T, T), np.float32)), BF16)
    pq, pk = _placement_matrices(heads, D // heads)
    S = heads * HEAD_SLOT
    row = lambda n: pl.BlockSpec((1, T, n), lambda bi, i: (bi, i, 0))
    return pl.pallas_call(
        functools.partial(_fox_proj_body, heads=heads),
        grid=(B, L // T),
        in_specs=[row(D), _resident(w_cat.shape), _resident((1, LANES)),
                  pl.BlockSpec((1, 1, LANES), lambda bi, i: (bi, 0, 0)),
                  _resident((T, T)), _resident((LANES, S)), _resident((LANES, S))],
        out_specs=[row(D), row(D), row(heads), row(S), row(S), row(D)],
        out_shape=[jax.ShapeDtypeStruct((B, L, D), F32), jax.ShapeDtypeStruct((B, L, D), F32),
                   jax.ShapeDtypeStruct((B, L, heads), F32),
                   jax.ShapeDtypeStruct((B, L, S), BF16), jax.ShapeDtypeStruct((B, L, S), BF16),
                   jax.ShapeDtypeStruct((B, L, D), BF16)],
        scratch_shapes=[pltpu.VMEM((1, LANES), F32)],
        compiler_params=_compiler_params(("parallel", "arbitrary")),
        name="fox_proj",
    )(x, w_cat, bf_pad, c0, tri, pq, pk)


def _fox_cache_body(k_ref, v_ref, lf_ref, tri_ref, pk_ref, ks_ref, vb_ref, cend_ref, carry_ref, *, heads):
    l = pl.program_id(1)

    @pl.when(l == 0)
    def _():
        carry_ref[...] = jnp.zeros(carry_ref.shape, F32)

    D = k_ref.shape[2]
    c = _dot_exact_lhs01(tri_ref[...], lf_ref[0]) + carry_ref[...]
    last = c[c.shape[0] - 1:, :]
    carry_ref[...] = last
    cend_ref[0] = last
    parts = _bias_parts(c, heads)
    ks_ref[0] = _slotted(k_ref[0], _dot(parts, pk_ref[...]), D // heads)
    vb_ref[0] = v_ref[0].astype(BF16)


def _fox_cache(k, v, lf_pad, *, heads):
    B, P, D = k.shape
    T = _pick_tile(P, 256)
    tri = jnp.asarray(np.tril(np.ones((T, T), np.float32)), BF16)
    _, pk = _placement_matrices(heads, D // heads)
    S = heads * HEAD_SLOT
    row = lambda n: pl.BlockSpec((1, T, n), lambda bi, i: (bi, i, 0))
    return pl.pallas_call(
        functools.partial(_fox_cache_body, heads=heads),
        grid=(B, P // T),
        in_specs=[row(D), row(D), row(LANES), _resident((T, T)), _resident((LANES, S))],
        out_specs=[row(S), row(D), pl.BlockSpec((1, 1, LANES), lambda bi, i: (bi, 0, 0))],
        out_shape=[jax.ShapeDtypeStruct((B, P, S), BF16), jax.ShapeDtypeStruct((B, P, D), BF16),
                   jax.ShapeDtypeStruct((B, 1, LANES), F32)],
        scratch_shapes=[pltpu.VMEM((1, LANES), F32)],
        compiler_params=_compiler_params(("parallel", "arbitrary")),
        name="fox_cache",
    )(k, v, lf_pad, tri, pk)


def _fox_attn_body(q_ref, k_ref, v_ref, o_ref, *, tq, tk, q_off, head_dim):
    i = pl.program_id(2)
    q_lo = q_off + i * tq
    n_needed = (q_lo + tq + tk - 1) // tk
    n_full = (q_lo + 1) // tk
    outs = []
    for e in range(LANES // head_dim):
        q = q_ref[0, :, e * HEAD_SLOT:(e + 1) * HEAD_SLOT]

        def step(j, carry, masked, e=e, q=q):
            m, l, acc = carry
            start = pl.multiple_of(j * tk, tk)
            k = k_ref[0, pl.ds(start, tk), e * HEAD_SLOT:(e + 1) * HEAD_SLOT]
            v = v_ref[0, pl.ds(start, tk), :]
            s = _dot_nt(q, k)
            if masked:
                qpos = q_lo + lax.broadcasted_iota(jnp.int32, (tq, 1), 0)
                kpos = start + lax.broadcasted_iota(jnp.int32, (1, tk), 1)
                s = jnp.where(kpos <= qpos, s, MASK_VALUE)
            m_new = jnp.maximum(m, jnp.max(s, -1, keepdims=True))
            p = jnp.exp(s - m_new)
            a = jnp.exp(m - m_new)
            l = a * l + jnp.sum(p, -1, keepdims=True)
            acc = a * acc + _dot(p.astype(BF16), v)
            return m_new, l, acc

        carry = (jnp.full((tq, 1), MASK_VALUE, F32), jnp.zeros((tq, 1), F32), jnp.zeros((tq, LANES), F32))
        carry = lax.fori_loop(0, n_needed - n_full,
                              lambda t, c: step(n_needed - 1 - t, c, True), carry)
        carry = lax.fori_loop(0, n_full, lambda t, c: step(n_full - 1 - t, c, False), carry)
        _, l, acc = carry
        outs.append(acc / l)
    lane = lax.broadcasted_iota(jnp.int32, (tq, LANES), 1)
    o_ref[0] = jnp.where(lane < head_dim, outs[0], outs[1]).astype(BF16)


def _fox_attn(qs, ks, vb, *, q_off, head_dim):
    B, Lq, S = qs.shape
    Lk = ks.shape[1]
    D = vb.shape[2]
    tq = _pick_tile(Lq, 512)
    tk = _pick_tile(Lk, 512)
    assert tk % tq == 0 and q_off % tq == 0 and q_off + Lq <= Lk and LANES // head_dim == 2
    return pl.pallas_call(
        functools.partial(_fox_attn_body, tq=tq, tk=tk, q_off=q_off, head_dim=head_dim),
        grid=(B, D // LANES, Lq // tq),
        in_specs=[pl.BlockSpec((1, tq, 2 * HEAD_SLOT), lambda bi, p, i: (bi, i, p)),
                  pl.BlockSpec((1, Lk, 2 * HEAD_SLOT), lambda bi, p, i: (bi, 0, p)),
                  pl.BlockSpec((1, Lk, LANES), lambda bi, p, i: (bi, 0, p))],
        out_specs=pl.BlockSpec((1, tq, LANES), lambda bi, p, i: (bi, i, p)),
        out_shape=jax.ShapeDtypeStruct((B, Lq, D), BF16),
        compiler_params=_compiler_params(("parallel", "parallel", "arbitrary")),
        name="fox_attn",
    )(qs, ks, vb)


def _fox_weights(w_in, b_f, heads):
    D = w_in.shape[0]
    scale = (D // heads) ** -0.5
    assert scale == 2.0 ** round(np.log2(scale))
    wf = jnp.pad(w_in[:, 3 * D:], ((0, 0), (0, LANES - heads)))
    w_cat = jnp.concatenate([w_in[:, :D] * scale, w_in[:, D:3 * D], wf], axis=1).astype(BF16)
    bf_pad = jnp.pad(b_f, (0, LANES - heads)).reshape(1, LANES)
    return w_cat, bf_pad


def _sigmoid(x):
    return 1.0 / (1.0 + jnp.exp(-x))


def _hgrn_body(x_ref, win_ref, wo_ref, lbraw_ref, ng_ref, g_ref, b_ref, s0_ref, tri_ref,
               y_ref, sfin_ref, st_ref, qg_ref, kt_ref, kh_ref, qi_ref, vb_ref, eb_ref, o_ref,
               *, alpha, layer, T, heads):
    l = pl.program_id(1)
    D = x_ref.shape[2]
    hd = D // heads
    n_sub = CHUNK // SUBLANES
    n_chunk = T // CHUNK

    @pl.when(l == 0)
    def _():
        for h in range(heads):
            st_ref[h] = s0_ref[0, h].T

    lbr = lbraw_ref[...]
    e = jnp.exp(lbr - jnp.max(lbr, 0, keepdims=True))
    pr = e / jnp.sum(e, 0, keepdims=True)
    lb = jnp.zeros((1, D), F32)
    for r in range(1, layer + 1):
        lb = lb + pr[r:r + 1, :]

    x = x_ref[0]
    hp = _dot(x.astype(BF16), win_ref[...])
    qr, fr, v, gate = hp[:, :D], hp[:, D:2 * D], hp[:, 2 * D:3 * D], hp[:, 3 * D:]
    q = qr * _sigmoid(qr)
    logf = jnp.log(lb + (1.0 - lb) * _sigmoid(fr))
    kk = (1.0 - lb) * _sigmoid(-fr)
    b = _dot_exact_lhs01(tri_ref[...], logf)

    shape3 = (T // SUBLANES, SUBLANES, D)
    b3 = b.reshape(shape3)
    lf3 = logf.reshape(shape3)
    b_end = jnp.broadcast_to(b3[:, SUBLANES - 1:, :], shape3).reshape(T, D)
    b_start = jnp.broadcast_to(b3[:, :1, :] - lf3[:, :1, :], shape3).reshape(T, D)
    bc = b.reshape(n_chunk, CHUNK, D)
    b_last = jnp.broadcast_to(bc[:, CHUNK - 1:, :], bc.shape).reshape(T, D)

    qt = q * jnp.exp(b - b_start)
    dm = jnp.exp(b_end - b_start)
    qg_ref[0] = qt.astype(BF16)
    for gp in range(1, n_sub - 1):
        shifted = jnp.concatenate([jnp.ones((gp * SUBLANES, D), F32), dm[:T - gp * SUBLANES, :]], 0)
        qt = qt * shifted
        qg_ref[gp] = qt.astype(BF16)
    kt_ref[...] = (kk * jnp.exp(b_end - b)).astype(BF16)
    kh_ref[...] = (kk * jnp.exp(b_last - b)).astype(BF16)
    qi_ref[...] = (q * jnp.exp(b)).astype(BF16)
    vb_ref[...] = v.astype(BF16)
    eb_ref[...] = jnp.exp(b_last)

    q3 = q.reshape(shape3)
    k3 = kk.reshape(shape3)
    v3 = v.reshape(shape3)
    sub = lax.broadcasted_iota(jnp.int32, (T // SUBLANES, SUBLANES, 1), 1)
    od = [jnp.zeros((T // SUBLANES, SUBLANES, hd), F32) for _ in range(heads)]
    for d in range(SUBLANES):
        kr = k3 if d == 0 else pltpu.roll(k3, d, axis=1)
        br = b3 if d == 0 else pltpu.roll(b3, d, axis=1)
        vr = v3 if d == 0 else pltpu.roll(v3, d, axis=1)
        w = q3 * kr * jnp.exp(jnp.minimum(b3 - br, 0.0))
        for h in range(heads):
            sl = slice(h * hd, (h + 1) * hd)
            a = jnp.sum(w[:, :, sl], -1, keepdims=True)
            od[h] = od[h] + jnp.where(sub >= d, a, 0.0) * vr[:, :, sl]
    for h in range(heads):
        o_ref[:, h * hd:(h + 1) * hd] = od[h].reshape(T, hd)

    rsub = lax.broadcasted_iota(jnp.int32, (CHUNK, CHUNK), 0) // SUBLANES
    csub = lax.broadcasted_iota(jnp.int32, (CHUNK, CHUNK), 1) // SUBLANES
    gap = rsub - csub

    def chunk_step(c, _):
        rows = pl.ds(pl.multiple_of(c * CHUNK, CHUNK), CHUNK)
        for h in range(heads):
            sl = slice(h * hd, (h + 1) * hd)
            kt = kt_ref[rows, sl]
            vb = vb_ref[rows, sl]
            a_off = jnp.zeros((CHUNK, CHUNK), F32)
            for gp in range(n_sub - 1):
                a_g = _dot_nt(qg_ref[gp, rows, sl], kt)
                a_off = a_off + jnp.where(gap == gp + 1, a_g, 0.0)
            st = st_ref[h]
            o_h = _dot(a_off.astype(BF16), vb) + _dot_nt(qi_ref[rows, sl], st.astype(BF16))
            o_ref[rows, sl] = o_ref[rows, sl] + o_h
            u_t = _dot_tn(vb, kh_ref[rows, sl])
            st_ref[h] = st * eb_ref[pl.ds(c * CHUNK, 1), sl] + u_t
        return 0

    lax.fori_loop(0, n_chunk, chunk_step, 0)

    @pl.when(l == pl.num_programs(1) - 1)
    def _():
        for h in range(heads):
            sfin_ref[0, h] = st_ref[h].T

    outs = []
    for h in range(heads):
        oh = o_ref[:, h * hd:(h + 1) * hd]
        outs.append(oh * lax.rsqrt(jnp.mean(oh * oh, -1, keepdims=True) + RMS_EPS))
    o = jnp.concatenate(outs, -1) * ng_ref[...] * (gate * _sigmoid(gate))
    y = _dot(o.astype(BF16), wo_ref[...])
    y_ref[0] = _layer_norm(alpha * x + y, g_ref[...], b_ref[...])


def _hgrn_layer(x, s0, w_in, w_o, lb_raw, norm_g, g, b, *, alpha, layer):
    B, L, D = x.shape
    heads = D // HGRN_HEAD_DIM
    hd = HGRN_HEAD_DIM
    T = _pick_tile(L, 256)
    assert T % CHUNK == 0
    tri = jnp.asarray(np.kron(np.eye(T // CHUNK, dtype=np.float32), np.tril(np.ones((CHUNK, CHUNK), np.float32))), BF16)
    n_gap = CHUNK // SUBLANES - 1
    return pl.pallas_call(
        functools.partial(_hgrn_body, alpha=alpha, layer=layer, T=T, heads=heads),
        grid=(B, L // T),
        in_specs=[pl.BlockSpec((1, T, D), lambda bi, i: (bi, i, 0)),
                  _resident(w_in.shape), _resident(w_o.shape), _resident(lb_raw.shape),
                  _resident((1, D)), _resident((1, D)), _resident((1, D)),
                  pl.BlockSpec((1, heads, hd, hd), lambda bi, i: (bi, 0, 0, 0)),
                  _resident((T, T))],
        out_specs=[pl.BlockSpec((1, T, D), lambda bi, i: (bi, i, 0)),
                   pl.BlockSpec((1, heads, hd, hd), lambda bi, i: (bi, 0, 0, 0))],
        out_shape=[jax.ShapeDtypeStruct((B, L, D), F32), jax.ShapeDtypeStruct((B, heads, hd, hd), F32)],
        scratch_shapes=[pltpu.VMEM((heads, hd, hd), F32),
                        pltpu.VMEM((n_gap, T, D), BF16),
                        pltpu.VMEM((T, D), BF16),
                        pltpu.VMEM((T, D), BF16),
                        pltpu.VMEM((T, D), BF16),
                        pltpu.VMEM((T, D), BF16),
                        pltpu.VMEM((T, D), F32),
                        pltpu.VMEM((T, D), F32)],
        compiler_params=_compiler_params(("parallel", "arbitrary")),
        name="hgrn_layer",
    )(x, w_in, w_o, lb_raw, norm_g, g, b, s0, tri)


def kernel(x_prompt, x_sample, cache_fox_k, cache_fox_v, cache_fox_logf, state_pool, state_hgrn, pool_w, pool_scale, fox_w_in, fox_b_f, fox_w_o, hgrn_w_in, hgrn_lb_raw, hgrn_norm_g, hgrn_w_o, ln_mix_g, ln_mix_b, w_up, w_down, ln_ffn_g, ln_ffn_b):
    depth = w_up.shape[0]
    alpha = float((2 * depth) ** 0.25)
    B, L, D = x_prompt.shape
    Bs, Ls, _ = x_sample.shape
    past = cache_fox_k.shape[2]
    heads = FOX_HEADS
    hd = D // heads
    pool_state = state_pool.shape[2]
    row = lambda a: a.reshape(1, D)

    xp, xs = x_prompt, x_sample
    pool_p, pool_s = [], []
    fk_p, fv_p, fl_p, fk_s, fv_s, fl_s = [], [], [], [], [], []
    hg_p, hg_s = [], []
    for i in range(depth):
        kind, j = i % N_MIXERS, i // N_MIXERS
        g_mix, b_mix = row(ln_mix_g[i]), row(ln_mix_b[i])
        if kind == 0:
            w = pool_w[j].astype(BF16)
            scale = row(pool_scale[j])
            hist_p = jnp.zeros((B, POOL_HIST, D), F32)
            hist_s = jnp.pad(state_pool[j], ((0, 0), (POOL_HIST - pool_state, 0), (0, 0)))
            pool_p.append(xp[:, L - pool_state:])
            pool_s.append(xs[:, Ls - pool_state:])
            xp = _pool_ln(xp, hist_p, w, scale, g_mix, b_mix, alpha=alpha, pos0=0)
            xs = _pool_ln(xs, hist_s, w, scale, g_mix, b_mix, alpha=alpha, pos0=past)
        elif kind == 1:
            w_cat, bf_pad = _fox_weights(fox_w_in[j], fox_b_f[j], heads)
            w_o = fox_w_o[j].astype(BF16)
            k, v, lf, qs, ks, vb = _fox_proj(xp, w_cat, bf_pad, jnp.zeros((B, 1, LANES), F32), heads=heads)
            o = _fox_attn(qs, ks, vb, q_off=0, head_dim=hd)
            xp = _proj_ln(o.reshape(B * L, D), w_o, xp.reshape(B * L, D), g_mix, b_mix, alpha=alpha).reshape(B, L, D)
            fk_p.append(k.reshape(B, L, heads, hd))
            fv_p.append(v.reshape(B, L, heads, hd))
            fl_p.append(lf)
            lf_pad = jnp.pad(cache_fox_logf[j], ((0, 0), (0, 0), (0, LANES - heads)))
            ks_c, vb_c, c_end = _fox_cache(cache_fox_k[j].reshape(Bs, past, D), cache_fox_v[j].reshape(Bs, past, D),
                                           lf_pad, heads=heads)
            k, v, lf, qs, ks, vb = _fox_proj(xs, w_cat, bf_pad, c_end, heads=heads)
            tail = (-(past + Ls)) % LANES
            ks_all = jnp.pad(jnp.concatenate([ks_c, ks], 1), ((0, 0), (0, tail), (0, 0)))
            vb_all = jnp.pad(jnp.concatenate([vb_c, vb], 1), ((0, 0), (0, tail), (0, 0)))
            o = _fox_attn(qs, ks_all, vb_all, q_off=past, head_dim=hd)
            xs = _proj_ln(o.reshape(Bs * Ls, D), w_o, xs.reshape(Bs * Ls, D), g_mix, b_mix, alpha=alpha).reshape(Bs, Ls, D)
            fk_s.append(k.reshape(Bs, Ls, heads, hd))
            fv_s.append(v.reshape(Bs, Ls, heads, hd))
            fl_s.append(lf)
        else:
            w_in = hgrn_w_in[j].astype(BF16)
            w_o = hgrn_w_o[j].astype(BF16)
            ng = row(hgrn_norm_g[j])
            s0 = jnp.zeros((B,) + state_hgrn.shape[2:], F32)
            xp, sp = _hgrn_layer(xp, s0, w_in, w_o, hgrn_lb_raw, ng, g_mix, b_mix, alpha=alpha, layer=i)
            xs, ss = _hgrn_layer(xs, state_hgrn[j], w_in, w_o, hgrn_lb_raw, ng, g_mix, b_mix, alpha=alpha, layer=i)
            hg_p.append(sp)
            hg_s.append(ss)
        wu, wd = w_up[i].astype(BF16), w_down[i].astype(BF16)
        g_ffn, b_ffn = row(ln_ffn_g[i]), row(ln_ffn_b[i])
        xp = _mlp_ln(xp.reshape(B * L, D), wu, wd, g_ffn, b_ffn, alpha=alpha).reshape(B, L, D)
        xs = _mlp_ln(xs.reshape(Bs * Ls, D), wu, wd, g_ffn, b_ffn, alpha=alpha).reshape(Bs, Ls, D)
    return (xp, xs, jnp.stack(pool_p), jnp.stack(pool_s),
            jnp.stack(fk_p), jnp.stack(fv_p), jnp.stack(fl_p),
            jnp.stack(fk_s), jnp.stack(fv_s), jnp.stack(fl_s),
            jnp.stack(hg_p), jnp.stack(hg_s))
```

```python
import functools

import jax
import jax.numpy as jnp
import numpy as np
from jax import lax
from jax.experimental import pallas as pl
from jax.experimental.pallas import tpu as pltpu

F32 = jnp.float32
BF16 = jnp.bfloat16

CHUNK = 64
N_MIXERS = 3
POOL_WINDOWS = (2, 4, 8, 16)
POOL_HIST = 16
FOX_HEADS = 16
HGRN_HEAD_DIM = 128
LN_EPS = 1e-5
RMS_EPS = 1e-6

LANES = 128
SUBLANES = 8
VMEM_LIMIT_BYTES = 56 * 1024 * 1024

HEAD_SLOT = LANES
MASK_VALUE = -1e30
EXP_UNDERFLOW = 90.0
NORM_SLACK = 1.02


def _compiler_params(semantics):
    return pltpu.CompilerParams(dimension_semantics=semantics, vmem_limit_bytes=VMEM_LIMIT_BYTES)


def _resident(shape):
    return pl.BlockSpec(shape, lambda *_: (0,) * len(shape), pipeline_mode=pl.Buffered(1))


def _pick_tile(n, pref):
    t = min(n, pref)
    while n % t:
        t //= 2
    return t


def _layer_norm(z, g, b):
    mu = jnp.mean(z, -1, keepdims=True)
    zc = z - mu
    var = jnp.mean(zc * zc, -1, keepdims=True)
    return zc * lax.rsqrt(var + LN_EPS) * g + b


def _split3(x):
    hi = x.astype(BF16)
    r1 = x - hi.astype(F32)
    mid = r1.astype(BF16)
    lo = (r1 - mid.astype(F32)).astype(BF16)
    return hi, mid, lo


def _dot(a, b):
    return jnp.dot(a, b, preferred_element_type=F32)


def _dot_nt(a, b):
    return lax.dot_general(a, b, (((1,), (1,)), ((), ())), preferred_element_type=F32)


def _dot_tn(a, b):
    return lax.dot_general(a, b, (((0,), (0,)), ((), ())), preferred_element_type=F32)


def _dot_exact_lhs01(m01, x):
    hi, mid, lo = _split3(x)
    return _dot(m01, hi) + _dot(m01, mid) + _dot(m01, lo)


def _mlp_ln_body(x_ref, wup_ref, wdn_ref, g_ref, b_ref, o_ref, *, alpha, ff_chunk):
    x = x_ref[...]
    xb = x.astype(BF16)
    d_ff = wup_ref.shape[1]
    acc = jnp.zeros(x.shape, F32)
    for c in range(d_ff // ff_chunk):
        sl = slice(c * ff_chunk, (c + 1) * ff_chunk)
        h = jnp.maximum(_dot(xb, wup_ref[:, sl]), 0.0)
        acc = acc + _dot((h * h).astype(BF16), wdn_ref[sl, :])
    o_ref[...] = _layer_norm(alpha * x + acc, g_ref[...], b_ref[...])


def _mlp_ln(x2, wup, wdn, g, b, *, alpha):
    R, D = x2.shape
    d_ff = wup.shape[1]
    tm = _pick_tile(R, 512)
    return pl.pallas_call(
        functools.partial(_mlp_ln_body, alpha=alpha, ff_chunk=_pick_tile(d_ff, 512)),
        grid=(R // tm,),
        in_specs=[pl.BlockSpec((tm, D), lambda i: (i, 0)),
                  _resident((D, d_ff)), _resident((d_ff, D)),
                  _resident((1, D)), _resident((1, D))],
        out_specs=pl.BlockSpec((tm, D), lambda i: (i, 0)),
        out_shape=jax.ShapeDtypeStruct((R, D), F32),
        compiler_params=_compiler_params(("parallel",)),
        name="mlp_ln",
    )(x2, wup, wdn, g, b)


def _proj_ln_body(a_ref, w_ref, x_ref, g_ref, b_ref, o_ref, *, alpha):
    y = _dot(a_ref[...], w_ref[...])
    o_ref[...] = _layer_norm(alpha * x_ref[...] + y, g_ref[...], b_ref[...])


def _proj_ln(a2, w, x2, g, b, *, alpha):
    R, D = x2.shape
    tm = _pick_tile(R, 512)
    return pl.pallas_call(
        functools.partial(_proj_ln_body, alpha=alpha),
        grid=(R // tm,),
        in_specs=[pl.BlockSpec((tm, D), lambda i: (i, 0)),
                  _resident((D, D)),
                  pl.BlockSpec((tm, D), lambda i: (i, 0)),
                  _resident((1, D)), _resident((1, D))],
        out_specs=pl.BlockSpec((tm, D), lambda i: (i, 0)),
        out_shape=jax.ShapeDtypeStruct((R, D), F32),
        compiler_params=_compiler_params(("parallel",)),
        name="proj_ln",
    )(a2, w, x2, g, b)


def _pool_ln_body(x_ref, xprev_ref, hist_ref, w_ref, scale_ref, g_ref, b_ref, o_ref, xa_ref, *, alpha, pos0, tm):
    i = pl.program_id(1)
    x = x_ref[0]
    group = x.shape[1] // len(POOL_WINDOWS)
    xa_ref[0:POOL_HIST, :] = jnp.where(i == 0, hist_ref[0], xprev_ref[0])
    xa_ref[POOL_HIST:, :] = x
    n_avail = lax.broadcasted_iota(jnp.int32, (tm, 1), 0) + (i * tm + pos0 + 1)
    ys = []
    for gi, w in enumerate(POOL_WINDOWS):
        sl = slice(gi * group, (gi + 1) * group)
        s = xa_ref[:, sl]
        sh = 1
        while sh < w:
            s = s + pltpu.roll(s, sh, axis=0)
            sh *= 2
        cnt = jnp.minimum(n_avail, w).astype(F32)
        pooled = s[POOL_HIST:, :] / cnt - x[:, sl]
        ys.append(_dot(pooled.astype(BF16), w_ref[gi]))
    y = jnp.concatenate(ys, -1) * scale_ref[...]
    o_ref[0] = _layer_norm(alpha * x + y, g_ref[...], b_ref[...])


def _pool_ln(x, hist, w, scale, g, b, *, alpha, pos0):
    B, L, D = x.shape
    n_g, group = w.shape[0], w.shape[1]
    tm = _pick_tile(L, 512)
    assert tm % POOL_HIST == 0 and max(POOL_WINDOWS) <= POOL_HIST
    per = tm // POOL_HIST
    return pl.pallas_call(
        functools.partial(_pool_ln_body, alpha=alpha, pos0=pos0, tm=tm),
        grid=(B, L // tm),
        in_specs=[pl.BlockSpec((1, tm, D), lambda bi, i: (bi, i, 0)),
                  pl.BlockSpec((1, POOL_HIST, D), lambda bi, i: (bi, jnp.maximum(i * per - 1, 0), 0)),
                  pl.BlockSpec((1, POOL_HIST, D), lambda bi, i: (bi, 0, 0)),
                  _resident((n_g, group, group)),
                  _resident((1, D)), _resident((1, D)), _resident((1, D))],
        out_specs=pl.BlockSpec((1, tm, D), lambda bi, i: (bi, i, 0)),
        out_shape=jax.ShapeDtypeStruct((B, L, D), F32),
        scratch_shapes=[pltpu.VMEM((tm + POOL_HIST, D), F32)],
        compiler_params=_compiler_params(("parallel", "arbitrary")),
        name="pool_ln",
    )(x, x, hist, w, scale, g, b)


def _placement_matrices(heads, head_dim):
    assert 3 * heads < LANES and head_dim + 6 <= HEAD_SLOT
    pq = np.zeros((LANES, heads * HEAD_SLOT), np.float32)
    pk = np.zeros((LANES, heads * HEAD_SLOT), np.float32)
    one = 3 * heads
    for h in range(heads):
        base = h * HEAD_SLOT + head_dim
        for p in range(3):
            pq[p * heads + h, base + p] = 1.0
            pq[one, base + 3 + p] = 1.0
            pk[one, base + p] = 1.0
            pk[p * heads + h, base + 3 + p] = -1.0
    return jnp.asarray(pq, BF16), jnp.asarray(pk, BF16)


def _bias_parts(c, heads):
    hi, mid, lo = _split3(c)
    lane = lax.broadcasted_iota(jnp.int32, c.shape, 1)
    parts = (hi.astype(F32) + pltpu.roll(mid.astype(F32), heads, axis=1)
             + pltpu.roll(lo.astype(F32), 2 * heads, axis=1) + jnp.where(lane == 3 * heads, 1.0, 0.0))
    return parts.astype(BF16)


def _slotted(nat, aug, head_dim):
    T = nat.shape[0]
    per = LANES // head_dim
    assert per == 2
    lane = lax.broadcasted_iota(jnp.int32, (T, LANES), 1)
    low = lane < head_dim
    out = []
    for p in range(nat.shape[1] // LANES):
        pair = nat[:, p * LANES:(p + 1) * LANES]
        swapped = pltpu.roll(pair, head_dim, axis=1)
        out.append(jnp.where(low, pair, aug[:, (2 * p) * HEAD_SLOT:(2 * p + 1) * HEAD_SLOT]))
        out.append(jnp.where(low, swapped, aug[:, (2 * p + 1) * HEAD_SLOT:(2 * p + 2) * HEAD_SLOT]))
    return jnp.concatenate(out, -1).astype(BF16)


def _log_sigmoid(x):
    return jnp.minimum(x, 0.0) - jnp.log(1.0 + jnp.exp(-jnp.abs(x)))


def _key_norm_matrix(heads, head_dim):
    g = np.zeros((heads * head_dim, LANES), np.float32)
    for h in range(heads):
        g[h * head_dim:(h + 1) * head_dim, h] = 1.0
    return jnp.asarray(g, BF16)


def _max_sq_key_norm(k, gn):
    return jnp.max(_dot((k * k).astype(BF16), gn), 0, keepdims=True)


def _fox_proj_body(x_ref, w_ref, bf_ref, c0_ref, tri_ref, pq_ref, pk_ref, gn_ref,
                   k_ref, v_ref, lf_ref, c_ref, kn_ref, qs_ref, ks_ref, vb_ref, carry_ref, *, heads):
    l = pl.program_id(1)
    D = x_ref.shape[2]

    @pl.when(l == 0)
    def _():
        carry_ref[...] = c0_ref[0]

    h = _dot(x_ref[0].astype(BF16), w_ref[...])
    q, k, v = h[:, :D], h[:, D:2 * D], h[:, 2 * D:3 * D]
    fr = h[:, 3 * D:] + bf_ref[...]
    lane = lax.broadcasted_iota(jnp.int32, fr.shape, 1)
    logf = jnp.where(lane < heads, _log_sigmoid(fr), 0.0)
    c = _dot_exact_lhs01(tri_ref[...], logf) + carry_ref[...]
    carry_ref[...] = c[c.shape[0] - 1:, :]
    parts = _bias_parts(c, heads)
    hd = D // heads
    k_ref[0] = k
    v_ref[0] = v
    vb_ref[0] = v.astype(BF16)
    lf_ref[0] = logf[:, :heads]
    c_ref[0] = c[:, :heads]
    kn_ref[0, 0] = _max_sq_key_norm(k, gn_ref[...])
    qs_ref[0] = _slotted(q, _dot(parts, pq_ref[...]), hd)
    ks_ref[0] = _slotted(k, _dot(parts, pk_ref[...]), hd)


def _fox_proj(x, w_cat, bf_pad, c0, *, heads):
    B, L, D = x.shape
    T = _pick_tile(L, 256)
    tri = jnp.asarray(np.tril(np.ones((T, T), np.float32)), BF16)
    pq, pk = _placement_matrices(heads, D // heads)
    gn = _key_norm_matrix(heads, D // heads)
    S = heads * HEAD_SLOT
    row = lambda n: pl.BlockSpec((1, T, n), lambda bi, i: (bi, i, 0))
    return pl.pallas_call(
        functools.partial(_fox_proj_body, heads=heads),
        grid=(B, L // T),
        in_specs=[row(D), _resident(w_cat.shape), _resident((1, LANES)),
                  pl.BlockSpec((1, 1, LANES), lambda bi, i: (bi, 0, 0)),
                  _resident((T, T)), _resident((LANES, S)), _resident((LANES, S)), _resident((D, LANES))],
        out_specs=[row(D), row(D), row(heads), row(heads),
                   pl.BlockSpec((1, 1, 1, LANES), lambda bi, i: (bi, i, 0, 0)),
                   row(S), row(S), row(D)],
        out_shape=[jax.ShapeDtypeStruct((B, L, D), F32), jax.ShapeDtypeStruct((B, L, D), F32),
                   jax.ShapeDtypeStruct((B, L, heads), F32), jax.ShapeDtypeStruct((B, L, heads), F32),
                   jax.ShapeDtypeStruct((B, L // T, 1, LANES), F32),
                   jax.ShapeDtypeStruct((B, L, S), BF16), jax.ShapeDtypeStruct((B, L, S), BF16),
                   jax.ShapeDtypeStruct((B, L, D), BF16)],
        scratch_shapes=[pltpu.VMEM((1, LANES), F32)],
        compiler_params=_compiler_params(("parallel", "arbitrary")),
        name="fox_proj",
    )(x, w_cat, bf_pad, c0, tri, pq, pk, gn)


def _fox_cache_body(k_ref, v_ref, lf_ref, tri_ref, pk_ref, gn_ref,
                    ks_ref, vb_ref, c_ref, kn_ref, cend_ref, carry_ref, *, heads):
    l = pl.program_id(1)

    @pl.when(l == 0)
    def _():
        carry_ref[...] = jnp.zeros(carry_ref.shape, F32)

    D = k_ref.shape[2]
    k = k_ref[0]
    c = _dot_exact_lhs01(tri_ref[...], lf_ref[0]) + carry_ref[...]
    last = c[c.shape[0] - 1:, :]
    carry_ref[...] = last
    cend_ref[0] = last
    c_ref[0] = c[:, :heads]
    kn_ref[0, 0] = _max_sq_key_norm(k, gn_ref[...])
    parts = _bias_parts(c, heads)
    ks_ref[0] = _slotted(k, _dot(parts, pk_ref[...]), D // heads)
    vb_ref[0] = v_ref[0].astype(BF16)


def _fox_cache(k, v, lf_pad, *, heads):
    B, P, D = k.shape
    T = _pick_tile(P, 256)
    tri = jnp.asarray(np.tril(np.ones((T, T), np.float32)), BF16)
    _, pk = _placement_matrices(heads, D // heads)
    gn = _key_norm_matrix(heads, D // heads)
    S = heads * HEAD_SLOT
    row = lambda n: pl.BlockSpec((1, T, n), lambda bi, i: (bi, i, 0))
    return pl.pallas_call(
        functools.partial(_fox_cache_body, heads=heads),
        grid=(B, P // T),
        in_specs=[row(D), row(D), row(LANES), _resident((T, T)), _resident((LANES, S)), _resident((D, LANES))],
        out_specs=[row(S), row(D), row(heads),
                   pl.BlockSpec((1, 1, 1, LANES), lambda bi, i: (bi, i, 0, 0)),
                   pl.BlockSpec((1, 1, LANES), lambda bi, i: (bi, 0, 0))],
        out_shape=[jax.ShapeDtypeStruct((B, P, S), BF16), jax.ShapeDtypeStruct((B, P, D), BF16),
                   jax.ShapeDtypeStruct((B, P, heads), F32),
                   jax.ShapeDtypeStruct((B, P // T, 1, LANES), F32),
                   jax.ShapeDtypeStruct((B, 1, LANES), F32)],
        scratch_shapes=[pltpu.VMEM((1, LANES), F32)],
        compiler_params=_compiler_params(("parallel", "arbitrary")),
        name="fox_cache",
    )(k, v, lf_pad, tri, pk, gn)


def _fox_attn_body(q_ref, k_ref, v_ref, cend_ref, kn_ref, o_ref, s_ref, p_ref, m_ref, l_ref, acc_ref,
                   *, tq, tk, q_off, head_dim, strip):
    i = pl.program_id(2)
    q_lo = q_off + i * tq
    j_diag = (q_lo + tq - 1) // tk
    lane = lax.broadcasted_iota(jnp.int32, (tq, LANES), 1)
    blk = lax.broadcasted_iota(jnp.int32, (1, LANES), 1)
    n_head = LANES // head_dim
    qs = [q_ref[0, :, e * HEAD_SLOT:(e + 1) * HEAD_SLOT] for e in range(n_head)]

    def step(j, masked):
        start = pl.multiple_of(j * tk, tk)
        v = v_ref[0, pl.ds(start, tk), :]
        m_news = []
        for e in range(n_head):
            k = k_ref[0, pl.ds(start, tk), e * HEAD_SLOT:(e + 1) * HEAD_SLOT]
            s = _dot_nt(qs[e], k)
            if masked:
                qpos = q_lo + lax.broadcasted_iota(jnp.int32, (tq, 1), 0)
                kpos = start + lax.broadcasted_iota(jnp.int32, (1, tk), 1)
                s = jnp.where(kpos <= qpos, s, MASK_VALUE)
            s_ref[e] = s
            m_news.append(jnp.broadcast_to(jnp.max(s, -1, keepdims=True), (tq, LANES)))
        for e in range(n_head):
            m = m_ref[e]
            m_new = jnp.maximum(m, m_news[e])
            m_ref[e] = m_new
            a = jnp.exp(m - m_new)
            for r in range(tq // strip):
                rows = slice(r * strip, (r + 1) * strip)
                part = jnp.zeros((strip, LANES), F32)
                for t in range(tk // LANES):
                    cols = slice(t * LANES, (t + 1) * LANES)
                    p = jnp.exp(s_ref[e, rows, cols] - m_new[rows])
                    p_ref[e, rows, cols] = p.astype(BF16)
                    part = part + p
                l_ref[e, rows, :] = a[rows] * l_ref[e, rows, :] + part
            acc_ref[e] = a * acc_ref[e] + _dot(p_ref[e], v)

    m_ref[...] = jnp.full(m_ref.shape, MASK_VALUE, F32)
    l_ref[...] = jnp.zeros(l_ref.shape, F32)
    acc_ref[...] = jnp.zeros(acc_ref.shape, F32)
    step(j_diag, True)
    n_live = jnp.int32(0)
    for e in range(n_head):
        qf = qs[e].astype(F32)
        q_norm = jnp.sqrt(jnp.sum(jnp.where(lane < head_dim, qf * qf, 0.0), -1, keepdims=True))
        c_q = jnp.sum(jnp.where((lane >= head_dim) & (lane < head_dim + 3), qf, 0.0), -1, keepdims=True)
        k_norm = jnp.sqrt(kn_ref[0, 0, e:e + 1, 0:1]) * NORM_SLACK
        reach = jnp.max(q_norm * k_norm + c_q - m_ref[e, :, 0:1], 0, keepdims=True)
        live = (blk < j_diag) & (cend_ref[0, 0, e:e + 1, :] <= reach + EXP_UNDERFLOW)
        n_live = jnp.maximum(n_live, jnp.sum(jnp.where(live, 1, 0)))

    def body(t, _):
        step(j_diag - 1 - t, False)
        return 0

    lax.fori_loop(0, n_live, body, 0)
    outs = [acc_ref[e] / jnp.sum(l_ref[e], -1, keepdims=True) for e in range(n_head)]
    o_ref[0] = jnp.where(lane < head_dim, outs[0], outs[1]).astype(BF16)


def _fox_attn(qs, ks, vb, c_keys, kn2, *, q_off, head_dim):
    B, Lq, S = qs.shape
    Lk = ks.shape[1]
    D = vb.shape[2]
    heads = D // head_dim
    tq = _pick_tile(Lq, 512)
    tk = Lk if Lk <= 2048 else _pick_tile(Lk, 512)
    n_blk = Lk // tk
    assert tk % tq == 0 and q_off % tq == 0 and q_off + Lq <= Lk and LANES // head_dim == 2 and n_blk <= LANES
    cend = jnp.swapaxes(c_keys[:, tk - 1::tk, :], 1, 2)
    cend = jnp.pad(cend, ((0, 0), (0, 0), (0, LANES - n_blk))).reshape(B, heads // 2, 2, LANES)
    knb = jnp.broadcast_to(kn2.reshape(B, heads // 2, 2, 1), (B, heads // 2, 2, LANES))
    pair = lambda: pl.BlockSpec((1, 1, 2, LANES), lambda bi, p, i: (bi, p, 0, 0))
    return pl.pallas_call(
        functools.partial(_fox_attn_body, tq=tq, tk=tk, q_off=q_off, head_dim=head_dim, strip=min(tq, 32)),
        grid=(B, D // LANES, Lq // tq),
        in_specs=[pl.BlockSpec((1, tq, 2 * HEAD_SLOT), lambda bi, p, i: (bi, i, p)),
                  pl.BlockSpec((1, Lk, 2 * HEAD_SLOT), lambda bi, p, i: (bi, 0, p)),
                  pl.BlockSpec((1, Lk, LANES), lambda bi, p, i: (bi, 0, p)),
                  pair(), pair()],
        out_specs=pl.BlockSpec((1, tq, LANES), lambda bi, p, i: (bi, i, p)),
        out_shape=jax.ShapeDtypeStruct((B, Lq, D), BF16),
        scratch_shapes=[pltpu.VMEM((2, tq, tk), F32),
                        pltpu.VMEM((2, tq, tk), BF16),
                        pltpu.VMEM((2, tq, LANES), F32),
                        pltpu.VMEM((2, tq, LANES), F32),
                        pltpu.VMEM((2, tq, LANES), F32)],
        compiler_params=_compiler_params(("parallel", "parallel", "arbitrary")),
        name="fox_attn",
    )(qs, ks, vb, cend, knb)


def _fox_weights(w_in, b_f, heads):
    D = w_in.shape[0]
    scale = (D // heads) ** -0.5
    assert scale == 2.0 ** round(np.log2(scale))
    wf = jnp.pad(w_in[:, 3 * D:], ((0, 0), (0, LANES - heads)))
    w_cat = jnp.concatenate([w_in[:, :D] * scale, w_in[:, D:3 * D], wf], axis=1).astype(BF16)
    bf_pad = jnp.pad(b_f, (0, LANES - heads)).reshape(1, LANES)
    return w_cat, bf_pad


def _sigmoid(x):
    return 1.0 / (1.0 + jnp.exp(-x))


def _hgrn_body(x_ref, win_ref, wo_ref, lbraw_ref, ng_ref, g_ref, b_ref, s0_ref, tri_ref,
               y_ref, sfin_ref, st_ref, qg_ref, kt_ref, kh_ref, qi_ref, vb_ref, eb_ref, o_ref,
               *, alpha, layer, T, heads):
    l = pl.program_id(1)
    D = x_ref.shape[2]
    hd = D // heads
    n_sub = CHUNK // SUBLANES
    n_chunk = T // CHUNK

    @pl.when(l == 0)
    def _():
        for h in range(heads):
            st_ref[h] = s0_ref[0, h].T

    lbr = lbraw_ref[...]
    e = jnp.exp(lbr - jnp.max(lbr, 0, keepdims=True))
    pr = e / jnp.sum(e, 0, keepdims=True)
    lb = jnp.zeros((1, D), F32)
    for r in range(1, layer + 1):
        lb = lb + pr[r:r + 1, :]

    x = x_ref[0]
    hp = _dot(x.astype(BF16), win_ref[...])
    qr, fr, v, gate = hp[:, :D], hp[:, D:2 * D], hp[:, 2 * D:3 * D], hp[:, 3 * D:]
    q = qr * _sigmoid(qr)
    logf = jnp.log(lb + (1.0 - lb) * _sigmoid(fr))
    kk = (1.0 - lb) * _sigmoid(-fr)
    b = _dot_exact_lhs01(tri_ref[...], logf)

    shape3 = (T // SUBLANES, SUBLANES, D)
    b3 = b.reshape(shape3)
    lf3 = logf.reshape(shape3)
    b_end = jnp.broadcast_to(b3[:, SUBLANES - 1:, :], shape3).reshape(T, D)
    b_start = jnp.broadcast_to(b3[:, :1, :] - lf3[:, :1, :], shape3).reshape(T, D)
    bc = b.reshape(n_chunk, CHUNK, D)
    b_last = jnp.broadcast_to(bc[:, CHUNK - 1:, :], bc.shape).reshape(T, D)

    qt = q * jnp.exp(b - b_start)
    dm = jnp.exp(b_end - b_start)
    qg_ref[0] = qt.astype(BF16)
    for gp in range(1, n_sub - 1):
        shifted = jnp.concatenate([jnp.ones((gp * SUBLANES, D), F32), dm[:T - gp * SUBLANES, :]], 0)
        qt = qt * shifted
        qg_ref[gp] = qt.astype(BF16)
    kt_ref[...] = (kk * jnp.exp(b_end - b)).astype(BF16)
    kh_ref[...] = (kk * jnp.exp(b_last - b)).astype(BF16)
    qi_ref[...] = (q * jnp.exp(b)).astype(BF16)
    vb_ref[...] = v.astype(BF16)
    eb_ref[...] = jnp.exp(b_last)

    q3 = q.reshape(shape3)
    k3 = kk.reshape(shape3)
    v3 = v.reshape(shape3)
    sub = lax.broadcasted_iota(jnp.int32, (T // SUBLANES, SUBLANES, 1), 1)
    od = [jnp.zeros((T // SUBLANES, SUBLANES, hd), F32) for _ in range(heads)]
    for d in range(SUBLANES):
        kr = k3 if d == 0 else pltpu.roll(k3, d, axis=1)
        br = b3 if d == 0 else pltpu.roll(b3, d, axis=1)
        vr = v3 if d == 0 else pltpu.roll(v3, d, axis=1)
        w = q3 * kr * jnp.exp(jnp.minimum(b3 - br, 0.0))
        for h in range(heads):
            sl = slice(h * hd, (h + 1) * hd)
            a = jnp.sum(w[:, :, sl], -1, keepdims=True)
            od[h] = od[h] + jnp.where(sub >= d, a, 0.0) * vr[:, :, sl]
    for h in range(heads):
        o_ref[:, h * hd:(h + 1) * hd] = od[h].reshape(T, hd)

    rsub = lax.broadcasted_iota(jnp.int32, (CHUNK, CHUNK), 0) // SUBLANES
    csub = lax.broadcasted_iota(jnp.int32, (CHUNK, CHUNK), 1) // SUBLANES
    gap = rsub - csub

    def chunk_step(c, _):
        rows = pl.ds(pl.multiple_of(c * CHUNK, CHUNK), CHUNK)
        for h in range(heads):
            sl = slice(h * hd, (h + 1) * hd)
            kt = kt_ref[rows, sl]
            vb = vb_ref[rows, sl]
            a_off = jnp.zeros((CHUNK, CHUNK), F32)
            for gp in range(n_sub - 1):
                a_g = _dot_nt(qg_ref[gp, rows, sl], kt)
                a_off = a_off + jnp.where(gap == gp + 1, a_g, 0.0)
            st = st_ref[h]
            o_h = _dot(a_off.astype(BF16), vb) + _dot_nt(qi_ref[rows, sl], st.astype(BF16))
            o_ref[rows, sl] = o_ref[rows, sl] + o_h
            u_t = _dot_tn(vb, kh_ref[rows, sl])
            st_ref[h] = st * eb_ref[pl.ds(c * CHUNK, 1), sl] + u_t
        return 0

    lax.fori_loop(0, n_chunk, chunk_step, 0)

    @pl.when(l == pl.num_programs(1) - 1)
    def _():
        for h in range(heads):
            sfin_ref[0, h] = st_ref[h].T

    outs = []
    for h in range(heads):
        oh = o_ref[:, h * hd:(h + 1) * hd]
        outs.append(oh * lax.rsqrt(jnp.mean(oh * oh, -1, keepdims=True) + RMS_EPS))
    o = jnp.concatenate(outs, -1) * ng_ref[...] * (gate * _sigmoid(gate))
    y = _dot(o.astype(BF16), wo_ref[...])
    y_ref[0] = _layer_norm(alpha * x + y, g_ref[...], b_ref[...])


def _hgrn_layer(x, s0, w_in, w_o, lb_raw, norm_g, g, b, *, alpha, layer):
    B, L, D = x.shape
    heads = D // HGRN_HEAD_DIM
    hd = HGRN_HEAD_DIM
    T = _pick_tile(L, 256)
    assert T % CHUNK == 0
    tri = jnp.asarray(np.kron(np.eye(T // CHUNK, dtype=np.float32), np.tril(np.ones((CHUNK, CHUNK), np.float32))), BF16)
    n_gap = CHUNK // SUBLANES - 1
    return pl.pallas_call(
        functools.partial(_hgrn_body, alpha=alpha, layer=layer, T=T, heads=heads),
        grid=(B, L // T),
        in_specs=[pl.BlockSpec((1, T, D), lambda bi, i: (bi, i, 0)),
                  _resident(w_in.shape), _resident(w_o.shape), _resident(lb_raw.shape),
                  _resident((1, D)), _resident((1, D)), _resident((1, D)),
                  pl.BlockSpec((1, heads, hd, hd), lambda bi, i: (bi, 0, 0, 0)),
                  _resident((T, T))],
        out_specs=[pl.BlockSpec((1, T, D), lambda bi, i: (bi, i, 0)),
                   pl.BlockSpec((1, heads, hd, hd), lambda bi, i: (bi, 0, 0, 0))],
        out_shape=[jax.ShapeDtypeStruct((B, L, D), F32), jax.ShapeDtypeStruct((B, heads, hd, hd), F32)],
        scratch_shapes=[pltpu.VMEM((heads, hd, hd), F32),
                        pltpu.VMEM((n_gap, T, D), BF16),
                        pltpu.VMEM((T, D), BF16),
                        pltpu.VMEM((T, D), BF16),
                        pltpu.VMEM((T, D), BF16),
                        pltpu.VMEM((T, D), BF16),
                        pltpu.VMEM((T, D), F32),
                        pltpu.VMEM((T, D), F32)],
        compiler_params=_compiler_params(("parallel", "arbitrary")),
        name="hgrn_layer",
    )(x, w_in, w_o, lb_raw, norm_g, g, b, s0, tri)


def kernel(x_prompt, x_sample, cache_fox_k, cache_fox_v, cache_fox_logf, state_pool, state_hgrn, pool_w, pool_scale, fox_w_in, fox_b_f, fox_w_o, hgrn_w_in, hgrn_lb_raw, hgrn_norm_g, hgrn_w_o, ln_mix_g, ln_mix_b, w_up, w_down, ln_ffn_g, ln_ffn_b):
    depth = w_up.shape[0]
    alpha = float((2 * depth) ** 0.25)
    B, L, D = x_prompt.shape
    Bs, Ls, _ = x_sample.shape
    past = cache_fox_k.shape[2]
    heads = FOX_HEADS
    hd = D // heads
    pool_state = state_pool.shape[2]
    row = lambda a: a.reshape(1, D)

    xp, xs = x_prompt, x_sample
    pool_p, pool_s = [], []
    fk_p, fv_p, fl_p, fk_s, fv_s, fl_s = [], [], [], [], [], []
    hg_p, hg_s = [], []
    for i in range(depth):
        kind, j = i % N_MIXERS, i // N_MIXERS
        g_mix, b_mix = row(ln_mix_g[i]), row(ln_mix_b[i])
        if kind == 0:
            w = pool_w[j].astype(BF16)
            scale = row(pool_scale[j])
            hist_p = jnp.zeros((B, POOL_HIST, D), F32)
            hist_s = jnp.pad(state_pool[j], ((0, 0), (POOL_HIST - pool_state, 0), (0, 0)))
            pool_p.append(xp[:, L - pool_state:])
            pool_s.append(xs[:, Ls - pool_state:])
            xp = _pool_ln(xp, hist_p, w, scale, g_mix, b_mix, alpha=alpha, pos0=0)
            xs = _pool_ln(xs, hist_s, w, scale, g_mix, b_mix, alpha=alpha, pos0=past)
        elif kind == 1:
            w_cat, bf_pad = _fox_weights(fox_w_in[j], fox_b_f[j], heads)
            w_o = fox_w_o[j].astype(BF16)
            k, v, lf, c, kn, qs, ks, vb = _fox_proj(xp, w_cat, bf_pad, jnp.zeros((B, 1, LANES), F32), heads=heads)
            o = _fox_attn(qs, ks, vb, c, jnp.max(kn, (1, 2))[:, :heads], q_off=0, head_dim=hd)
            xp = _proj_ln(o.reshape(B * L, D), w_o, xp.reshape(B * L, D), g_mix, b_mix, alpha=alpha).reshape(B, L, D)
            fk_p.append(k.reshape(B, L, heads, hd))
            fv_p.append(v.reshape(B, L, heads, hd))
            fl_p.append(lf)
            lf_pad = jnp.pad(cache_fox_logf[j], ((0, 0), (0, 0), (0, LANES - heads)))
            ks_c, vb_c, c_c, kn_c, c_end = _fox_cache(cache_fox_k[j].reshape(Bs, past, D),
                                                      cache_fox_v[j].reshape(Bs, past, D), lf_pad, heads=heads)
            k, v, lf, c, kn, qs, ks, vb = _fox_proj(xs, w_cat, bf_pad, c_end, heads=heads)
            tail = (-(past + Ls)) % LANES
            grow = ((0, 0), (0, tail), (0, 0))
            ks_all = jnp.pad(jnp.concatenate([ks_c, ks], 1), grow)
            vb_all = jnp.pad(jnp.concatenate([vb_c, vb], 1), grow)
            c_all = jnp.pad(jnp.concatenate([c_c, c], 1), grow, mode="edge")
            kn_all = jnp.maximum(jnp.max(kn_c, (1, 2)), jnp.max(kn, (1, 2)))[:, :heads]
            o = _fox_attn(qs, ks_all, vb_all, c_all, kn_all, q_off=past, head_dim=hd)
            xs = _proj_ln(o.reshape(Bs * Ls, D), w_o, xs.reshape(Bs * Ls, D), g_mix, b_mix, alpha=alpha).reshape(Bs, Ls, D)
            fk_s.append(k.reshape(Bs, Ls, heads, hd))
            fv_s.append(v.reshape(Bs, Ls, heads, hd))
            fl_s.append(lf)
        else:
            w_in = hgrn_w_in[j].astype(BF16)
            w_o = hgrn_w_o[j].astype(BF16)
            ng = row(hgrn_norm_g[j])
            s0 = jnp.zeros((B,) + state_hgrn.shape[2:], F32)
            xp, sp = _hgrn_layer(xp, s0, w_in, w_o, hgrn_lb_raw, ng, g_mix, b_mix, alpha=alpha, layer=i)
            xs, ss = _hgrn_layer(xs, state_hgrn[j], w_in, w_o, hgrn_lb_raw, ng, g_mix, b_mix, alpha=alpha, layer=i)
            hg_p.append(sp)
            hg_s.append(ss)
        wu, wd = w_up[i].astype(BF16), w_down[i].astype(BF16)
        g_ffn, b_ffn = row(ln_ffn_g[i]), row(ln_ffn_b[i])
        xp = _mlp_ln(xp.reshape(B * L, D), wu, wd, g_ffn, b_ffn, alpha=alpha).reshape(B, L, D)
        xs = _mlp_ln(xs.reshape(Bs * Ls, D), wu, wd, g_ffn, b_ffn, alpha=alpha).reshape(Bs, Ls, D)
    return (xp, xs, jnp.stack(pool_p), jnp.stack(pool_s),
            jnp.stack(fk_p), jnp.stack(fv_p), jnp.stack(fl_p),
            jnp.stack(fk_s), jnp.stack(fv_s), jnp.stack(fl_s),
            jnp.stack(hg_p), jnp.stack(hg_s))
```

```python
import functools

import jax
import jax.numpy as jnp
import numpy as np
from jax import lax
from jax.experimental import pallas as pl
from jax.experimental.pallas import tpu as pltpu

F32 = jnp.float32
BF16 = jnp.bfloat16

CHUNK = 64
N_MIXERS = 3
POOL_WINDOWS = (2, 4, 8, 16)
POOL_HIST = 16
FOX_HEADS = 16
HGRN_HEAD_DIM = 128
LN_EPS = 1e-5
RMS_EPS = 1e-6

LANES = 128
SUBLANES = 8
VMEM_LIMIT_BYTES = 56 * 1024 * 1024

HEAD_SLOT = LANES
MASK_VALUE = -1e30
LOG2E = float(np.log2(np.e))
EXP_UNDERFLOW = 90.0
NORM_SLACK = 1.02
DECAY_LIMIT = 60.0


def _compiler_params(semantics):
    return pltpu.CompilerParams(dimension_semantics=semantics, vmem_limit_bytes=VMEM_LIMIT_BYTES)


def _resident(shape):
    return pl.BlockSpec(shape, lambda *_: (0,) * len(shape), pipeline_mode=pl.Buffered(1))


def _pick_tile(n, pref):
    t = min(n, pref)
    while n % t:
        t //= 2
    return t


def _layer_norm(z, g, b):
    mu = jnp.mean(z, -1, keepdims=True)
    zc = z - mu
    var = jnp.mean(zc * zc, -1, keepdims=True)
    return zc * lax.rsqrt(var + LN_EPS) * g + b


def _split3(x):
    hi = x.astype(BF16)
    r1 = x - hi.astype(F32)
    mid = r1.astype(BF16)
    lo = (r1 - mid.astype(F32)).astype(BF16)
    return hi, mid, lo


def _dot(a, b):
    return jnp.dot(a, b, preferred_element_type=F32)


def _dot_nt(a, b):
    return lax.dot_general(a, b, (((1,), (1,)), ((), ())), preferred_element_type=F32)


def _dot_tn(a, b):
    return lax.dot_general(a, b, (((0,), (0,)), ((), ())), preferred_element_type=F32)


def _dot_exact_lhs01(m01, x, pieces=3):
    hi, mid, lo = _split3(x)
    y = _dot(m01, hi) + _dot(m01, mid)
    return y + _dot(m01, lo) if pieces == 3 else y


FF_CHUNK = 512


def _mlp_ln_value(x, wup_ref, wdn_ref, g, b, alpha):
    xb = x.astype(BF16)
    d_ff = wup_ref.shape[1]
    ff_chunk = _pick_tile(d_ff, FF_CHUNK)
    acc = jnp.zeros(x.shape, F32)
    for c in range(d_ff // ff_chunk):
        sl = slice(c * ff_chunk, (c + 1) * ff_chunk)
        h = jnp.maximum(_dot(xb, wup_ref[:, sl]), 0.0)
        acc = acc + _dot((h * h).astype(BF16), wdn_ref[sl, :])
    return _layer_norm(alpha * x + acc, g, b)


def _mlp_specs(D, d_ff):
    return [_resident((D, d_ff)), _resident((d_ff, D)), _resident((1, D)), _resident((1, D))]


def _mlp_ln_body(x_ref, wup_ref, wdn_ref, g_ref, b_ref, o_ref, *, alpha):
    o_ref[...] = _mlp_ln_value(x_ref[...], wup_ref, wdn_ref, g_ref[...], b_ref[...], alpha)


def _mlp_ln(x2, mlp, *, alpha):
    R, D = x2.shape
    tm = _pick_tile(R, 512)
    return pl.pallas_call(
        functools.partial(_mlp_ln_body, alpha=alpha),
        grid=(R // tm,),
        in_specs=[pl.BlockSpec((tm, D), lambda i: (i, 0))] + _mlp_specs(D, mlp[0].shape[1]),
        out_specs=pl.BlockSpec((tm, D), lambda i: (i, 0)),
        out_shape=jax.ShapeDtypeStruct((R, D), F32),
        compiler_params=_compiler_params(("parallel",)),
        name="mlp_ln",
    )(x2, *mlp)


def _proj_mlp_body(a_ref, w_ref, x_ref, g_ref, b_ref, wup_ref, wdn_ref, g2_ref, b2_ref, o_ref, *, alpha):
    y = _dot(a_ref[...], w_ref[...])
    x1 = _layer_norm(alpha * x_ref[...] + y, g_ref[...], b_ref[...])
    o_ref[...] = _mlp_ln_value(x1, wup_ref, wdn_ref, g2_ref[...], b2_ref[...], alpha)


def _proj_mlp(a2, w, x2, g, b, mlp, *, alpha):
    R, D = x2.shape
    tm = _pick_tile(R, 512)
    return pl.pallas_call(
        functools.partial(_proj_mlp_body, alpha=alpha),
        grid=(R // tm,),
        in_specs=[pl.BlockSpec((tm, D), lambda i: (i, 0)),
                  _resident((D, D)),
                  pl.BlockSpec((tm, D), lambda i: (i, 0)),
                  _resident((1, D)), _resident((1, D))] + _mlp_specs(D, mlp[0].shape[1]),
        out_specs=pl.BlockSpec((tm, D), lambda i: (i, 0)),
        out_shape=jax.ShapeDtypeStruct((R, D), F32),
        compiler_params=_compiler_params(("parallel",)),
        name="proj_mlp",
    )(a2, w, x2, g, b, *mlp)


def _pool_mlp_body(x_ref, xprev_ref, hist_ref, w_ref, scale_ref, g_ref, b_ref, wup_ref, wdn_ref, g2_ref, b2_ref,
                   o_ref, xa_ref, *, alpha, pos0, tm):
    i = pl.program_id(1)
    x = x_ref[0]
    group = x.shape[1] // len(POOL_WINDOWS)
    xa_ref[0:POOL_HIST, :] = jnp.where(i == 0, hist_ref[0], xprev_ref[0])
    xa_ref[POOL_HIST:, :] = x
    n_avail = lax.broadcasted_iota(jnp.int32, (tm, 1), 0) + (i * tm + pos0 + 1)
    ys = []
    for gi, w in enumerate(POOL_WINDOWS):
        sl = slice(gi * group, (gi + 1) * group)
        s = xa_ref[:, sl]
        sh = 1
        while sh < w:
            s = s + pltpu.roll(s, sh, axis=0)
            sh *= 2
        cnt = jnp.minimum(n_avail, w).astype(F32)
        pooled = s[POOL_HIST:, :] / cnt - x[:, sl]
        ys.append(_dot(pooled.astype(BF16), w_ref[gi]))
    y = jnp.concatenate(ys, -1) * scale_ref[...]
    x1 = _layer_norm(alpha * x + y, g_ref[...], b_ref[...])
    o_ref[0] = _mlp_ln_value(x1, wup_ref, wdn_ref, g2_ref[...], b2_ref[...], alpha)


def _pool_mlp(x, hist, w, scale, g, b, mlp, *, alpha, pos0):
    B, L, D = x.shape
    n_g, group = w.shape[0], w.shape[1]
    tm = _pick_tile(L, 512)
    assert tm % POOL_HIST == 0 and max(POOL_WINDOWS) <= POOL_HIST
    per = tm // POOL_HIST
    return pl.pallas_call(
        functools.partial(_pool_mlp_body, alpha=alpha, pos0=pos0, tm=tm),
        grid=(B, L // tm),
        in_specs=[pl.BlockSpec((1, tm, D), lambda bi, i: (bi, i, 0)),
                  pl.BlockSpec((1, POOL_HIST, D), lambda bi, i: (bi, jnp.maximum(i * per - 1, 0), 0)),
                  pl.BlockSpec((1, POOL_HIST, D), lambda bi, i: (bi, 0, 0)),
                  _resident((n_g, group, group)),
                  _resident((1, D)), _resident((1, D)), _resident((1, D))] + _mlp_specs(D, mlp[0].shape[1]),
        out_specs=pl.BlockSpec((1, tm, D), lambda bi, i: (bi, i, 0)),
        out_shape=jax.ShapeDtypeStruct((B, L, D), F32),
        scratch_shapes=[pltpu.VMEM((tm + POOL_HIST, D), F32)],
        compiler_params=_compiler_params(("parallel", "arbitrary")),
        name="pool_mlp",
    )(x, x, hist, w, scale, g, b, *mlp)


def _placement_matrices(heads, head_dim):
    assert 3 * heads < LANES and head_dim + 6 <= HEAD_SLOT
    pq = np.zeros((LANES, heads * HEAD_SLOT), np.float32)
    pk = np.zeros((LANES, heads * HEAD_SLOT), np.float32)
    one = 3 * heads
    for h in range(heads):
        base = h * HEAD_SLOT + head_dim
        for p in range(3):
            pq[p * heads + h, base + p] = 1.0
            pq[one, base + 3 + p] = 1.0
            pk[one, base + p] = 1.0
            pk[p * heads + h, base + 3 + p] = -1.0
    return jnp.asarray(pq, BF16), jnp.asarray(pk, BF16)


def _bias_parts(c, heads):
    hi, mid, lo = _split3(c)
    lane = lax.broadcasted_iota(jnp.int32, c.shape, 1)
    parts = (hi.astype(F32) + pltpu.roll(mid.astype(F32), heads, axis=1)
             + pltpu.roll(lo.astype(F32), 2 * heads, axis=1) + jnp.where(lane == 3 * heads, 1.0, 0.0))
    return parts.astype(BF16)


def _slotted(nat, aug, head_dim):
    T = nat.shape[0]
    per = LANES // head_dim
    assert per == 2
    lane = lax.broadcasted_iota(jnp.int32, (T, LANES), 1)
    low = lane < head_dim
    out = []
    for p in range(nat.shape[1] // LANES):
        pair = nat[:, p * LANES:(p + 1) * LANES]
        swapped = pltpu.roll(pair, head_dim, axis=1)
        out.append(jnp.where(low, pair, aug[:, (2 * p) * HEAD_SLOT:(2 * p + 1) * HEAD_SLOT]))
        out.append(jnp.where(low, swapped, aug[:, (2 * p + 1) * HEAD_SLOT:(2 * p + 2) * HEAD_SLOT]))
    return jnp.concatenate(out, -1).astype(BF16)


def _log_sigmoid(x):
    return jnp.minimum(x, 0.0) - jnp.log(1.0 + jnp.exp(-jnp.abs(x)))


def _key_norm_matrix(heads, head_dim):
    g = np.zeros((heads * head_dim, LANES), np.float32)
    for h in range(heads):
        g[h * head_dim:(h + 1) * head_dim, h] = 1.0
    return jnp.asarray(g, BF16)


def _max_sq_key_norm(k, gn):
    return jnp.max(_dot((k * k).astype(BF16), gn), 0, keepdims=True)


def _fox_proj_body(x_ref, w_ref, bf_ref, c0_ref, tri_ref, pq_ref, pk_ref, gn_ref,
                   k_ref, v_ref, lf_ref, c_ref, kn_ref, qs_ref, ks_ref, vb_ref, carry_ref, *, heads):
    l = pl.program_id(1)
    D = x_ref.shape[2]

    @pl.when(l == 0)
    def _():
        carry_ref[...] = c0_ref[0]

    h = _dot(x_ref[0].astype(BF16), w_ref[...])
    q, k, v = h[:, :D], h[:, D:2 * D], h[:, 2 * D:3 * D]
    fr = h[:, 3 * D:] + bf_ref[...]
    lane = lax.broadcasted_iota(jnp.int32, fr.shape, 1)
    logf = jnp.where(lane < heads, _log_sigmoid(fr), 0.0)
    c = _dot_exact_lhs01(tri_ref[...], logf) + carry_ref[...]
    carry_ref[...] = c[c.shape[0] - 1:, :]
    parts = _bias_parts(c * LOG2E, heads)
    hd = D // heads
    for hh in range(heads):
        k_ref.at[0][pl.ds(hh, k.shape[0], stride=heads), :] = k[:, hh * hd:(hh + 1) * hd]
        v_ref.at[0][pl.ds(hh, v.shape[0], stride=heads), :] = v[:, hh * hd:(hh + 1) * hd]
    vb_ref[0] = v.astype(BF16)
    lf_ref[0] = logf[:, :heads]
    c_ref[0] = c[:, :heads]
    kn_ref[0, 0] = _max_sq_key_norm(k, gn_ref[...])
    qs_ref[0] = _slotted(q, _dot(parts, pq_ref[...]), hd)
    ks_ref[0] = _slotted(k, _dot(parts, pk_ref[...]), hd)


def _fox_proj(x, w_cat, bf_pad, c0, *, heads):
    B, L, D = x.shape
    T = _pick_tile(L, 256)
    tri = jnp.asarray(np.tril(np.ones((T, T), np.float32)), BF16)
    pq, pk = _placement_matrices(heads, D // heads)
    gn = _key_norm_matrix(heads, D // heads)
    S = heads * HEAD_SLOT
    row = lambda n: pl.BlockSpec((1, T, n), lambda bi, i: (bi, i, 0))
    kv_out = pl.BlockSpec((1, T * heads, D // heads), lambda bi, i: (bi, i, 0))
    return pl.pallas_call(
        functools.partial(_fox_proj_body, heads=heads),
        grid=(B, L // T),
        in_specs=[row(D), _resident(w_cat.shape), _resident((1, LANES)),
                  pl.BlockSpec((1, 1, LANES), lambda bi, i: (bi, 0, 0)),
                  _resident((T, T)), _resident((LANES, S)), _resident((LANES, S)), _resident((D, LANES))],
        out_specs=[kv_out, kv_out, row(heads), row(heads),
                   pl.BlockSpec((1, 1, 1, LANES), lambda bi, i: (bi, i, 0, 0)),
                   row(S), row(S), row(D)],
        out_shape=[jax.ShapeDtypeStruct((B, L * heads, D // heads), F32),
                   jax.ShapeDtypeStruct((B, L * heads, D // heads), F32),
                   jax.ShapeDtypeStruct((B, L, heads), F32), jax.ShapeDtypeStruct((B, L, heads), F32),
                   jax.ShapeDtypeStruct((B, L // T, 1, LANES), F32),
                   jax.ShapeDtypeStruct((B, L, S), BF16), jax.ShapeDtypeStruct((B, L, S), BF16),
                   jax.ShapeDtypeStruct((B, L, D), BF16)],
        scratch_shapes=[pltpu.VMEM((1, LANES), F32)],
        compiler_params=_compiler_params(("parallel", "arbitrary")),
        name="fox_proj",
    )(x, w_cat, bf_pad, c0, tri, pq, pk, gn)


def _fox_cache_body(k_ref, v_ref, lf_ref, tri_ref, pk_ref, gn_ref,
                    ks_ref, vb_ref, c_ref, kn_ref, cend_ref, carry_ref, *, heads):
    l = pl.program_id(1)

    @pl.when(l == 0)
    def _():
        carry_ref[...] = jnp.zeros(carry_ref.shape, F32)

    D = k_ref.shape[2]
    k = k_ref[0]
    c = _dot_exact_lhs01(tri_ref[...], lf_ref[0]) + carry_ref[...]
    last = c[c.shape[0] - 1:, :]
    carry_ref[...] = last
    cend_ref[0] = last
    c_ref[0] = c[:, :heads]
    kn_ref[0, 0] = _max_sq_key_norm(k, gn_ref[...])
    parts = _bias_parts(c * LOG2E, heads)
    ks_ref[0] = _slotted(k, _dot(parts, pk_ref[...]), D // heads)
    vb_ref[0] = v_ref[0].astype(BF16)


def _fox_cache(k, v, lf_pad, *, heads):
    B, P, D = k.shape
    T = _pick_tile(P, 256)
    tri = jnp.asarray(np.tril(np.ones((T, T), np.float32)), BF16)
    _, pk = _placement_matrices(heads, D // heads)
    gn = _key_norm_matrix(heads, D // heads)
    S = heads * HEAD_SLOT
    row = lambda n: pl.BlockSpec((1, T, n), lambda bi, i: (bi, i, 0))
    return pl.pallas_call(
        functools.partial(_fox_cache_body, heads=heads),
        grid=(B, P // T),
        in_specs=[row(D), row(D), row(LANES), _resident((T, T)), _resident((LANES, S)), _resident((D, LANES))],
        out_specs=[row(S), row(D), row(heads),
                   pl.BlockSpec((1, 1, 1, LANES), lambda bi, i: (bi, i, 0, 0)),
                   pl.BlockSpec((1, 1, LANES), lambda bi, i: (bi, 0, 0))],
        out_shape=[jax.ShapeDtypeStruct((B, P, S), BF16), jax.ShapeDtypeStruct((B, P, D), BF16),
                   jax.ShapeDtypeStruct((B, P, heads), F32),
                   jax.ShapeDtypeStruct((B, P // T, 1, LANES), F32),
                   jax.ShapeDtypeStruct((B, 1, LANES), F32)],
        scratch_shapes=[pltpu.VMEM((1, LANES), F32)],
        compiler_params=_compiler_params(("parallel", "arbitrary")),
        name="fox_cache",
    )(k, v, lf_pad, tri, pk, gn)


def _fox_attn_body(q_ref, k_ref, v_ref, cend_ref, kn_ref, o_ref, s_ref, p_ref, m_ref, l_ref, acc_ref,
                   *, tq, tk, q_off, head_dim, strip):
    i = pl.program_id(2)
    q_lo = q_off + i * tq
    j_diag = (q_lo + tq - 1) // tk
    lane = lax.broadcasted_iota(jnp.int32, (tq, LANES), 1)
    blk = lax.broadcasted_iota(jnp.int32, (1, LANES), 1)
    n_head = LANES // head_dim
    qs = [q_ref[0, :, e * HEAD_SLOT:(e + 1) * HEAD_SLOT] for e in range(n_head)]

    def step(j, masked):
        start = pl.multiple_of(j * tk, tk)
        v = v_ref[0, pl.ds(start, tk), :]
        m_news = []
        for e in range(n_head):
            k = k_ref[0, pl.ds(start, tk), e * HEAD_SLOT:(e + 1) * HEAD_SLOT]
            s = _dot_nt(qs[e], k)
            if masked:
                qpos = q_lo + lax.broadcasted_iota(jnp.int32, (tq, 1), 0)
                kpos = start + lax.broadcasted_iota(jnp.int32, (1, tk), 1)
                s = jnp.where(kpos <= qpos, s, MASK_VALUE)
            s_ref[e] = s
            m_news.append(jnp.broadcast_to(jnp.max(s, -1, keepdims=True), (tq, LANES)))
        for e in range(n_head):
            m = m_ref[e]
            m_new = jnp.maximum(m, m_news[e])
            m_ref[e] = m_new
            a = jnp.exp2(m - m_new)
            for r in range(tq // strip):
                rows = slice(r * strip, (r + 1) * strip)
                part = jnp.zeros((strip, LANES), F32)
                for t in range(tk // LANES):
                    cols = slice(t * LANES, (t + 1) * LANES)
                    p = jnp.exp2(s_ref[e, rows, cols] - m_new[rows])
                    p_ref[e, rows, cols] = p.astype(BF16)
                    part = part + p
                l_ref[e, rows, :] = a[rows] * l_ref[e, rows, :] + part
            acc_ref[e] = a * acc_ref[e] + _dot(p_ref[e], v)

    m_ref[...] = jnp.full(m_ref.shape, MASK_VALUE, F32)
    l_ref[...] = jnp.zeros(l_ref.shape, F32)
    acc_ref[...] = jnp.zeros(acc_ref.shape, F32)
    step(j_diag, True)
    n_live = jnp.int32(0)
    for e in range(n_head):
        qf = qs[e].astype(F32)
        q_norm = jnp.sqrt(jnp.sum(jnp.where(lane < head_dim, qf * qf, 0.0), -1, keepdims=True))
        c_q = jnp.sum(jnp.where((lane >= head_dim) & (lane < head_dim + 3), qf, 0.0), -1, keepdims=True)
        k_norm = jnp.sqrt(kn_ref[0, 0, e:e + 1, 0:1]) * NORM_SLACK
        reach = jnp.max(q_norm * k_norm + c_q - m_ref[e, :, 0:1], 0, keepdims=True)
        live = (blk < j_diag) & (cend_ref[0, 0, e:e + 1, :] * LOG2E <= reach + EXP_UNDERFLOW * LOG2E)
        n_live = jnp.maximum(n_live, jnp.sum(jnp.where(live, 1, 0)))

    def body(t, _):
        step(j_diag - 1 - t, False)
        return 0

    lax.fori_loop(0, n_live, body, 0)
    outs = [acc_ref[e] / jnp.sum(l_ref[e], -1, keepdims=True) for e in range(n_head)]
    o_ref[0] = jnp.where(lane < head_dim, outs[0], outs[1]).astype(BF16)


def _fox_attn(qs, ks, vb, c_keys, kn2, *, q_off, head_dim):
    B, Lq, S = qs.shape
    Lk = ks.shape[1]
    D = vb.shape[2]
    heads = D // head_dim
    tq = _pick_tile(Lq, 512)
    tk = Lk if Lk <= 2048 else _pick_tile(Lk, 512)
    n_blk = Lk // tk
    assert tk % tq == 0 and q_off % tq == 0 and q_off + Lq <= Lk and LANES // head_dim == 2 and n_blk <= LANES
    cend = jnp.swapaxes(c_keys[:, tk - 1::tk, :], 1, 2)
    cend = jnp.pad(cend, ((0, 0), (0, 0), (0, LANES - n_blk))).reshape(B, heads // 2, 2, LANES)
    knb = jnp.broadcast_to(kn2.reshape(B, heads // 2, 2, 1), (B, heads // 2, 2, LANES))
    pair = lambda: pl.BlockSpec((1, 1, 2, LANES), lambda bi, p, i: (bi, p, 0, 0))
    return pl.pallas_call(
        functools.partial(_fox_attn_body, tq=tq, tk=tk, q_off=q_off, head_dim=head_dim, strip=min(tq, 32)),
        grid=(B, D // LANES, Lq // tq),
        in_specs=[pl.BlockSpec((1, tq, 2 * HEAD_SLOT), lambda bi, p, i: (bi, i, p)),
                  pl.BlockSpec((1, Lk, 2 * HEAD_SLOT), lambda bi, p, i: (bi, 0, p)),
                  pl.BlockSpec((1, Lk, LANES), lambda bi, p, i: (bi, 0, p)),
                  pair(), pair()],
        out_specs=pl.BlockSpec((1, tq, LANES), lambda bi, p, i: (bi, i, p)),
        out_shape=jax.ShapeDtypeStruct((B, Lq, D), BF16),
        scratch_shapes=[pltpu.VMEM((2, tq, tk), F32),
                        pltpu.VMEM((2, tq, tk), BF16),
                        pltpu.VMEM((2, tq, LANES), F32),
                        pltpu.VMEM((2, tq, LANES), F32),
                        pltpu.VMEM((2, tq, LANES), F32)],
        compiler_params=_compiler_params(("parallel", "parallel", "arbitrary")),
        name="fox_attn",
    )(qs, ks, vb, cend, knb)


def _fox_weights(w_in, b_f, heads):
    D = w_in.shape[0]
    scale = (D // heads) ** -0.5 * LOG2E
    wf = jnp.pad(w_in[:, 3 * D:], ((0, 0), (0, LANES - heads)))
    w_cat = jnp.concatenate([w_in[:, :D] * scale, w_in[:, D:3 * D], wf], axis=1).astype(BF16)
    bf_pad = jnp.pad(b_f, (0, LANES - heads)).reshape(1, LANES)
    return w_cat, bf_pad


def _sigmoid(x):
    return 1.0 / (1.0 + jnp.exp(-x))


def _hgrn_body(x_ref, win_ref, wo_ref, lbraw_ref, ng_ref, g_ref, b_ref, s0_ref, tri_ref,
               y_ref, sfin_ref, st_ref, qg_ref, kt_ref, kh_ref, qi_ref, vb_ref, eb_ref, o_ref,
               *, alpha, layer, T, heads):
    l = pl.program_id(1)
    D = x_ref.shape[2]
    hd = D // heads
    n_sub = CHUNK // SUBLANES
    n_chunk = T // CHUNK

    @pl.when(l == 0)
    def _():
        for h in range(heads):
            st_ref[h] = s0_ref[0, h].T

    lbr = lbraw_ref[...]
    e = jnp.exp(lbr - jnp.max(lbr, 0, keepdims=True))
    pr = e / jnp.sum(e, 0, keepdims=True)
    lb = jnp.zeros((1, D), F32)
    for r in range(1, layer + 1):
        lb = lb + pr[r:r + 1, :]

    x = x_ref[0]
    hp = _dot(x.astype(BF16), win_ref[...])
    qr, fr, v, gate = hp[:, :D], hp[:, D:2 * D], hp[:, 2 * D:3 * D], hp[:, 3 * D:]
    q = qr * _sigmoid(qr)
    logf = jnp.log(lb + (1.0 - lb) * _sigmoid(fr))
    kk = (1.0 - lb) * _sigmoid(-fr)
    b = _dot_exact_lhs01(tri_ref[...], logf, pieces=2)
    bc = b.reshape(n_chunk, CHUNK, D)
    b_last = jnp.broadcast_to(bc[:, CHUNK - 1:, :], bc.shape).reshape(T, D)

    kh_ref[...] = (kk * jnp.exp(b_last - b)).astype(BF16)
    qi_ref[...] = (q * jnp.exp(b)).astype(BF16)
    vb_ref[...] = v.astype(BF16)
    eb_ref[...] = jnp.exp(b_last)
    mild = jnp.min(b) >= -DECAY_LIMIT

    @pl.when(mild)
    def _():
        kt_ref[...] = (kk * jnp.exp(-b)).astype(BF16)
        row = lax.broadcasted_iota(jnp.int32, (T, T), 0)
        col = lax.broadcasted_iota(jnp.int32, (T, T), 1)
        shift = CHUNK.bit_length() - 1
        keep = (lax.shift_right_logical(row, shift) == lax.shift_right_logical(col, shift)) & (col <= row)
        for h in range(heads):
            sl = slice(h * hd, (h + 1) * hd)
            a = jnp.where(keep, _dot_nt(qi_ref[:, sl], kt_ref[:, sl]), 0.0)
            o_h = _dot(a.astype(BF16), vb_ref[:, sl])
            st = st_ref[h]
            inter = []
            for c in range(n_chunk):
                rc = slice(c * CHUNK, (c + 1) * CHUNK)
                inter.append(_dot_nt(qi_ref[rc, sl], st.astype(BF16)))
                st = st * eb_ref[c * CHUNK:c * CHUNK + 1, sl] + _dot_tn(vb_ref[rc, sl], kh_ref[rc, sl])
            st_ref[h] = st
            o_ref[:, sl] = o_h + jnp.concatenate(inter, 0)

    @pl.when(jnp.logical_not(mild))
    def _():
        shape3 = (T // SUBLANES, SUBLANES, D)
        b3 = b.reshape(shape3)
        lf3 = logf.reshape(shape3)
        b_end = jnp.broadcast_to(b3[:, SUBLANES - 1:, :], shape3).reshape(T, D)
        b_start = jnp.broadcast_to(b3[:, :1, :] - lf3[:, :1, :], shape3).reshape(T, D)
        qt = q * jnp.exp(jnp.minimum(b - b_start, 0.0))
        dm = jnp.exp(jnp.minimum(b_end - b_start, 0.0))
        qg_ref[0] = qt.astype(BF16)
        for gp in range(1, n_sub - 1):
            shifted = jnp.concatenate([jnp.ones((gp * SUBLANES, D), F32), dm[:T - gp * SUBLANES, :]], 0)
            qt = qt * shifted
            qg_ref[gp] = qt.astype(BF16)
        kt_ref[...] = (kk * jnp.exp(b_end - b)).astype(BF16)

        q3 = q.reshape(shape3)
        k3 = kk.reshape(shape3)
        v3 = v.reshape(shape3)
        sub = lax.broadcasted_iota(jnp.int32, (T // SUBLANES, SUBLANES, 1), 1)
        od = [jnp.zeros((T // SUBLANES, SUBLANES, hd), F32) for _ in range(heads)]
        for d in range(SUBLANES):
            kr = k3 if d == 0 else pltpu.roll(k3, d, axis=1)
            br = b3 if d == 0 else pltpu.roll(b3, d, axis=1)
            vr = v3 if d == 0 else pltpu.roll(v3, d, axis=1)
            w = q3 * kr * jnp.exp(jnp.minimum(b3 - br, 0.0))
            for h in range(heads):
                sl = slice(h * hd, (h + 1) * hd)
                a = jnp.sum(w[:, :, sl], -1, keepdims=True)
                od[h] = od[h] + jnp.where(sub >= d, a, 0.0) * vr[:, :, sl]
        for h in range(heads):
            o_ref[:, h * hd:(h + 1) * hd] = od[h].reshape(T, hd)

        rsub = lax.shift_right_logical(lax.broadcasted_iota(jnp.int32, (CHUNK, CHUNK), 0), 3)
        csub = lax.shift_right_logical(lax.broadcasted_iota(jnp.int32, (CHUNK, CHUNK), 1), 3)
        gap = rsub - csub

        def chunk_step(c, _):
            rows = pl.ds(pl.multiple_of(c * CHUNK, CHUNK), CHUNK)
            for h in range(heads):
                sl = slice(h * hd, (h + 1) * hd)
                kt = kt_ref[rows, sl]
                vb = vb_ref[rows, sl]
                a_off = jnp.zeros((CHUNK, CHUNK), F32)
                for gp in range(n_sub - 1):
                    a_g = _dot_nt(qg_ref[gp, rows, sl], kt)
                    a_off = a_off + jnp.where(gap == gp + 1, a_g, 0.0)
                st = st_ref[h]
                o_h = _dot(a_off.astype(BF16), vb) + _dot_nt(qi_ref[rows, sl], st.astype(BF16))
                o_ref[rows, sl] = o_ref[rows, sl] + o_h
                u_t = _dot_tn(vb, kh_ref[rows, sl])
                st_ref[h] = st * eb_ref[pl.ds(c * CHUNK, 1), sl] + u_t
            return 0

        lax.fori_loop(0, n_chunk, chunk_step, 0)

    @pl.when(l == pl.num_programs(1) - 1)
    def _():
        for h in range(heads):
            sfin_ref[0, h] = st_ref[h].T

    outs = []
    for h in range(heads):
        oh = o_ref[:, h * hd:(h + 1) * hd]
        outs.append(oh * lax.rsqrt(jnp.mean(oh * oh, -1, keepdims=True) + RMS_EPS))
    o = jnp.concatenate(outs, -1) * ng_ref[...] * (gate * _sigmoid(gate))
    y = _dot(o.astype(BF16), wo_ref[...])
    y_ref[0] = _layer_norm(alpha * x + y, g_ref[...], b_ref[...])


def _hgrn_layer(x, s0, w_in, w_o, lb_raw, norm_g, g, b, *, alpha, layer):
    B, L, D = x.shape
    heads = D // HGRN_HEAD_DIM
    hd = HGRN_HEAD_DIM
    T = _pick_tile(L, 256)
    assert T % CHUNK == 0
    tri = jnp.asarray(np.kron(np.eye(T // CHUNK, dtype=np.float32), np.tril(np.ones((CHUNK, CHUNK), np.float32))), BF16)
    n_gap = CHUNK // SUBLANES - 1
    return pl.pallas_call(
        functools.partial(_hgrn_body, alpha=alpha, layer=layer, T=T, heads=heads),
        grid=(B, L // T),
        in_specs=[pl.BlockSpec((1, T, D), lambda bi, i: (bi, i, 0)),
                  _resident(w_in.shape), _resident(w_o.shape), _resident(lb_raw.shape),
                  _resident((1, D)), _resident((1, D)), _resident((1, D)),
                  pl.BlockSpec((1, heads, hd, hd), lambda bi, i: (bi, 0, 0, 0)),
                  _resident((T, T))],
        out_specs=[pl.BlockSpec((1, T, D), lambda bi, i: (bi, i, 0)),
                   pl.BlockSpec((1, heads, hd, hd), lambda bi, i: (bi, 0, 0, 0))],
        out_shape=[jax.ShapeDtypeStruct((B, L, D), F32), jax.ShapeDtypeStruct((B, heads, hd, hd), F32)],
        scratch_shapes=[pltpu.VMEM((heads, hd, hd), F32),
                        pltpu.VMEM((n_gap, T, D), BF16),
                        pltpu.VMEM((T, D), BF16),
                        pltpu.VMEM((T, D), BF16),
                        pltpu.VMEM((T, D), BF16),
                        pltpu.VMEM((T, D), BF16),
                        pltpu.VMEM((T, D), F32),
                        pltpu.VMEM((T, D), F32)],
        compiler_params=_compiler_params(("parallel", "arbitrary")),
        name="hgrn_layer",
    )(x, w_in, w_o, lb_raw, norm_g, g, b, s0, tri)


def kernel(x_prompt, x_sample, cache_fox_k, cache_fox_v, cache_fox_logf, state_pool, state_hgrn, pool_w, pool_scale, fox_w_in, fox_b_f, fox_w_o, hgrn_w_in, hgrn_lb_raw, hgrn_norm_g, hgrn_w_o, ln_mix_g, ln_mix_b, w_up, w_down, ln_ffn_g, ln_ffn_b):
    depth = w_up.shape[0]
    alpha = float((2 * depth) ** 0.25)
    B, L, D = x_prompt.shape
    Bs, Ls, _ = x_sample.shape
    past = cache_fox_k.shape[2]
    heads = FOX_HEADS
    hd = D // heads
    pool_state = state_pool.shape[2]
    row = lambda a: a.reshape(1, D)

    xp, xs = x_prompt, x_sample
    pool_p, pool_s = [], []
    fk_p, fv_p, fl_p, fk_s, fv_s, fl_s = [], [], [], [], [], []
    hg_p, hg_s = [], []
    for i in range(depth):
        kind, j = i % N_MIXERS, i // N_MIXERS
        g_mix, b_mix = row(ln_mix_g[i]), row(ln_mix_b[i])
        mlp = (w_up[i].astype(BF16), w_down[i].astype(BF16), row(ln_ffn_g[i]), row(ln_ffn_b[i]))
        if kind == 0:
            w = pool_w[j].astype(BF16)
            scale = row(pool_scale[j])
            hist_p = jnp.zeros((B, POOL_HIST, D), F32)
            hist_s = jnp.pad(state_pool[j], ((0, 0), (POOL_HIST - pool_state, 0), (0, 0)))
            pool_p.append(xp[:, L - pool_state:])
            pool_s.append(xs[:, Ls - pool_state:])
            xp = _pool_mlp(xp, hist_p, w, scale, g_mix, b_mix, mlp, alpha=alpha, pos0=0)
            xs = _pool_mlp(xs, hist_s, w, scale, g_mix, b_mix, mlp, alpha=alpha, pos0=past)
        elif kind == 1:
            w_cat, bf_pad = _fox_weights(fox_w_in[j], fox_b_f[j], heads)
            w_o = fox_w_o[j].astype(BF16)
            k, v, lf, c, kn, qs, ks, vb = _fox_proj(xp, w_cat, bf_pad, jnp.zeros((B, 1, LANES), F32), heads=heads)
            o = _fox_attn(qs, ks, vb, c, jnp.max(kn, (1, 2))[:, :heads], q_off=0, head_dim=hd)
            xp = _proj_mlp(o.reshape(B * L, D), w_o, xp.reshape(B * L, D), g_mix, b_mix, mlp,
                           alpha=alpha).reshape(B, L, D)
            fk_p.append(k.reshape(B, L, heads, hd))
            fv_p.append(v.reshape(B, L, heads, hd))
            fl_p.append(lf)
            lf_pad = jnp.pad(cache_fox_logf[j], ((0, 0), (0, 0), (0, LANES - heads)))
            ks_c, vb_c, c_c, kn_c, c_end = _fox_cache(cache_fox_k[j].reshape(Bs, past, D),
                                                      cache_fox_v[j].reshape(Bs, past, D), lf_pad, heads=heads)
            k, v, lf, c, kn, qs, ks, vb = _fox_proj(xs, w_cat, bf_pad, c_end, heads=heads)
            tail = (-(past + Ls)) % LANES
            grow = ((0, 0), (0, tail), (0, 0))
            ks_all = jnp.pad(jnp.concatenate([ks_c, ks], 1), grow)
            vb_all = jnp.pad(jnp.concatenate([vb_c, vb], 1), grow)
            c_all = jnp.pad(jnp.concatenate([c_c, c], 1), grow, mode="edge")
            kn_all = jnp.maximum(jnp.max(kn_c, (1, 2)), jnp.max(kn, (1, 2)))[:, :heads]
            o = _fox_attn(qs, ks_all, vb_all, c_all, kn_all, q_off=past, head_dim=hd)
            xs = _proj_mlp(o.reshape(Bs * Ls, D), w_o, xs.reshape(Bs * Ls, D), g_mix, b_mix, mlp,
                           alpha=alpha).reshape(Bs, Ls, D)
            fk_s.append(k.reshape(Bs, Ls, heads, hd))
            fv_s.append(v.reshape(Bs, Ls, heads, hd))
            fl_s.append(lf)
        else:
            w_in = hgrn_w_in[j].astype(BF16)
            w_o = hgrn_w_o[j].astype(BF16)
            ng = row(hgrn_norm_g[j])
            s0 = jnp.zeros((B,) + state_hgrn.shape[2:], F32)
            xp, sp = _hgrn_layer(xp, s0, w_in, w_o, hgrn_lb_raw, ng, g_mix, b_mix, alpha=alpha, layer=i)
            xs, ss = _hgrn_layer(xs, state_hgrn[j], w_in, w_o, hgrn_lb_raw, ng, g_mix, b_mix, alpha=alpha, layer=i)
            hg_p.append(sp)
            hg_s.append(ss)
            xp = _mlp_ln(xp.reshape(B * L, D), mlp, alpha=alpha).reshape(B, L, D)
            xs = _mlp_ln(xs.reshape(Bs * Ls, D), mlp, alpha=alpha).reshape(Bs, Ls, D)
    return (xp, xs, jnp.stack(pool_p), jnp.stack(pool_s),
            jnp.stack(fk_p), jnp.stack(fv_p), jnp.stack(fl_p),
            jnp.stack(fk_s), jnp.stack(fv_s), jnp.stack(fl_s),
            jnp.stack(hg_p), jnp.stack(hg_s))
```

```python
import functools

import jax
import jax.numpy as jnp
import numpy as np
from jax import lax
from jax.experimental import pallas as pl
from jax.experimental.pallas import tpu as pltpu

F32 = jnp.float32
BF16 = jnp.bfloat16

CHUNK = 64
N_MIXERS = 3
POOL_WINDOWS = (2, 4, 8, 16)
POOL_HIST = 16
FOX_HEADS = 16
HGRN_HEAD_DIM = 128
LN_EPS = 1e-5
RMS_EPS = 1e-6

LANES = 128
SUBLANES = 8
VMEM_LIMIT_BYTES = 56 * 1024 * 1024

HEAD_SLOT = LANES
MASK_VALUE = -1e30
LOG2E = float(np.log2(np.e))
EXP_UNDERFLOW = 90.0
NORM_SLACK = 1.02
DECAY_LIMIT = 60.0


def _compiler_params(semantics):
    return pltpu.CompilerParams(dimension_semantics=semantics, vmem_limit_bytes=VMEM_LIMIT_BYTES)


def _resident(shape):
    return pl.BlockSpec(shape, lambda *_: (0,) * len(shape), pipeline_mode=pl.Buffered(1))


def _pick_tile(n, pref):
    t = min(n, pref)
    while n % t:
        t //= 2
    return t


def _layer_norm(z, g, b):
    mu = jnp.mean(z, -1, keepdims=True)
    zc = z - mu
    var = jnp.mean(zc * zc, -1, keepdims=True)
    return zc * lax.rsqrt(var + LN_EPS) * g + b


def _split3(x):
    hi = x.astype(BF16)
    r1 = x - hi.astype(F32)
    mid = r1.astype(BF16)
    lo = (r1 - mid.astype(F32)).astype(BF16)
    return hi, mid, lo


def _dot(a, b):
    return jnp.dot(a, b, preferred_element_type=F32)


def _dot_nt(a, b):
    return lax.dot_general(a, b, (((1,), (1,)), ((), ())), preferred_element_type=F32)


def _dot_tn(a, b):
    return lax.dot_general(a, b, (((0,), (0,)), ((), ())), preferred_element_type=F32)


def _dot_exact_lhs01(m01, x, pieces=3):
    hi, mid, lo = _split3(x)
    y = _dot(m01, hi) + _dot(m01, mid)
    return y + _dot(m01, lo) if pieces == 3 else y


FF_CHUNK = 512


def _mlp_ln_value(x, wup_ref, wdn_ref, g, b, alpha):
    xb = x.astype(BF16)
    d_ff = wup_ref.shape[1]
    ff_chunk = _pick_tile(d_ff, FF_CHUNK)
    acc = jnp.zeros(x.shape, F32)
    for c in range(d_ff // ff_chunk):
        sl = slice(c * ff_chunk, (c + 1) * ff_chunk)
        h = jnp.maximum(_dot(xb, wup_ref[:, sl]), 0.0)
        acc = acc + _dot((h * h).astype(BF16), wdn_ref[sl, :])
    return _layer_norm(alpha * x + acc, g, b)


def _mlp_specs(D, d_ff):
    return [_resident((D, d_ff)), _resident((d_ff, D)), _resident((1, D)), _resident((1, D))]


def _mlp_ln_body(x_ref, wup_ref, wdn_ref, g_ref, b_ref, o_ref, *, alpha):
    o_ref[...] = _mlp_ln_value(x_ref[...], wup_ref, wdn_ref, g_ref[...], b_ref[...], alpha)


def _mlp_ln(x2, mlp, *, alpha):
    R, D = x2.shape
    tm = _pick_tile(R, 512)
    return pl.pallas_call(
        functools.partial(_mlp_ln_body, alpha=alpha),
        grid=(R // tm,),
        in_specs=[pl.BlockSpec((tm, D), lambda i: (i, 0))] + _mlp_specs(D, mlp[0].shape[1]),
        out_specs=pl.BlockSpec((tm, D), lambda i: (i, 0)),
        out_shape=jax.ShapeDtypeStruct((R, D), F32),
        compiler_params=_compiler_params(("parallel",)),
        name="mlp_ln",
    )(x2, *mlp)


def _proj_mlp_body(a_ref, w_ref, x_ref, g_ref, b_ref, wup_ref, wdn_ref, g2_ref, b2_ref, o_ref, *, alpha):
    y = _dot(a_ref[...], w_ref[...])
    x1 = _layer_norm(alpha * x_ref[...] + y, g_ref[...], b_ref[...])
    o_ref[...] = _mlp_ln_value(x1, wup_ref, wdn_ref, g2_ref[...], b2_ref[...], alpha)


def _proj_mlp(a2, w, x2, g, b, mlp, *, alpha):
    R, D = x2.shape
    tm = _pick_tile(R, 512)
    return pl.pallas_call(
        functools.partial(_proj_mlp_body, alpha=alpha),
        grid=(R // tm,),
        in_specs=[pl.BlockSpec((tm, D), lambda i: (i, 0)),
                  _resident((D, D)),
                  pl.BlockSpec((tm, D), lambda i: (i, 0)),
                  _resident((1, D)), _resident((1, D))] + _mlp_specs(D, mlp[0].shape[1]),
        out_specs=pl.BlockSpec((tm, D), lambda i: (i, 0)),
        out_shape=jax.ShapeDtypeStruct((R, D), F32),
        compiler_params=_compiler_params(("parallel",)),
        name="proj_mlp",
    )(a2, w, x2, g, b, *mlp)


def _pool_mlp_body(x_ref, xprev_ref, hist_ref, w_ref, scale_ref, g_ref, b_ref, wup_ref, wdn_ref, g2_ref, b2_ref,
                   o_ref, xa_ref, *, alpha, pos0, tm):
    i = pl.program_id(1)
    x = x_ref[0]
    group = x.shape[1] // len(POOL_WINDOWS)
    xa_ref[0:POOL_HIST, :] = jnp.where(i == 0, hist_ref[0], xprev_ref[0])
    xa_ref[POOL_HIST:, :] = x
    n_avail = lax.broadcasted_iota(jnp.int32, (tm, 1), 0) + (i * tm + pos0 + 1)
    ys = []
    for gi, w in enumerate(POOL_WINDOWS):
        sl = slice(gi * group, (gi + 1) * group)
        s = xa_ref[:, sl]
        sh = 1
        while sh < w:
            s = s + pltpu.roll(s, sh, axis=0)
            sh *= 2
        cnt = jnp.minimum(n_avail, w).astype(F32)
        pooled = s[POOL_HIST:, :] / cnt - x[:, sl]
        ys.append(_dot(pooled.astype(BF16), w_ref[gi]))
    y = jnp.concatenate(ys, -1) * scale_ref[...]
    x1 = _layer_norm(alpha * x + y, g_ref[...], b_ref[...])
    o_ref[0] = _mlp_ln_value(x1, wup_ref, wdn_ref, g2_ref[...], b2_ref[...], alpha)


def _pool_mlp(x, hist, w, scale, g, b, mlp, *, alpha, pos0):
    B, L, D = x.shape
    n_g, group = w.shape[0], w.shape[1]
    tm = _pick_tile(L, 512)
    assert tm % POOL_HIST == 0 and max(POOL_WINDOWS) <= POOL_HIST
    per = tm // POOL_HIST
    return pl.pallas_call(
        functools.partial(_pool_mlp_body, alpha=alpha, pos0=pos0, tm=tm),
        grid=(B, L // tm),
        in_specs=[pl.BlockSpec((1, tm, D), lambda bi, i: (bi, i, 0)),
                  pl.BlockSpec((1, POOL_HIST, D), lambda bi, i: (bi, jnp.maximum(i * per - 1, 0), 0)),
                  pl.BlockSpec((1, POOL_HIST, D), lambda bi, i: (bi, 0, 0)),
                  _resident((n_g, group, group)),
                  _resident((1, D)), _resident((1, D)), _resident((1, D))] + _mlp_specs(D, mlp[0].shape[1]),
        out_specs=pl.BlockSpec((1, tm, D), lambda bi, i: (bi, i, 0)),
        out_shape=jax.ShapeDtypeStruct((B, L, D), F32),
        scratch_shapes=[pltpu.VMEM((tm + POOL_HIST, D), F32)],
        compiler_params=_compiler_params(("parallel", "arbitrary")),
        name="pool_mlp",
    )(x, x, hist, w, scale, g, b, *mlp)


def _placement_matrices(heads, head_dim):
    assert 3 * heads < LANES and head_dim + 6 <= HEAD_SLOT
    pq = np.zeros((LANES, heads * HEAD_SLOT), np.float32)
    pk = np.zeros((LANES, heads * HEAD_SLOT), np.float32)
    one = 3 * heads
    for h in range(heads):
        base = h * HEAD_SLOT + head_dim
        for p in range(3):
            pq[p * heads + h, base + p] = 1.0
            pq[one, base + 3 + p] = 1.0
            pk[one, base + p] = 1.0
            pk[p * heads + h, base + 3 + p] = -1.0
    return jnp.asarray(pq, BF16), jnp.asarray(pk, BF16)


def _bias_parts(c, heads):
    hi, mid, lo = _split3(c)
    lane = lax.broadcasted_iota(jnp.int32, c.shape, 1)
    parts = (hi.astype(F32) + pltpu.roll(mid.astype(F32), heads, axis=1)
             + pltpu.roll(lo.astype(F32), 2 * heads, axis=1) + jnp.where(lane == 3 * heads, 1.0, 0.0))
    return parts.astype(BF16)


def _slotted(nat, aug, head_dim):
    T = nat.shape[0]
    per = LANES // head_dim
    assert per == 2
    lane = lax.broadcasted_iota(jnp.int32, (T, LANES), 1)
    low = lane < head_dim
    out = []
    for p in range(nat.shape[1] // LANES):
        pair = nat[:, p * LANES:(p + 1) * LANES]
        swapped = pltpu.roll(pair, head_dim, axis=1)
        out.append(jnp.where(low, pair, aug[:, (2 * p) * HEAD_SLOT:(2 * p + 1) * HEAD_SLOT]))
        out.append(jnp.where(low, swapped, aug[:, (2 * p + 1) * HEAD_SLOT:(2 * p + 2) * HEAD_SLOT]))
    return jnp.concatenate(out, -1).astype(BF16)


def _log_sigmoid(x):
    return jnp.minimum(x, 0.0) - jnp.log(1.0 + jnp.exp(-jnp.abs(x)))


def _key_norm_matrix(heads, head_dim):
    g = np.zeros((heads * head_dim, LANES), np.float32)
    for h in range(heads):
        g[h * head_dim:(h + 1) * head_dim, h] = 1.0
    return jnp.asarray(g, BF16)


def _max_sq_key_norm(k, gn):
    return jnp.max(_dot((k * k).astype(BF16), gn), 0, keepdims=True)


def _fox_proj_body(x_ref, w_ref, bf_ref, c0_ref, tri_ref, pq_ref, pk_ref, gn_ref,
                   k_ref, v_ref, lf_ref, c_ref, kn_ref, qs_ref, ks_ref, vb_ref, carry_ref, *, heads):
    l = pl.program_id(1)
    D = x_ref.shape[2]

    @pl.when(l == 0)
    def _():
        carry_ref[...] = c0_ref[0]

    h = _dot(x_ref[0].astype(BF16), w_ref[...])
    q, k, v = h[:, :D], h[:, D:2 * D], h[:, 2 * D:3 * D]
    fr = h[:, 3 * D:] + bf_ref[...]
    lane = lax.broadcasted_iota(jnp.int32, fr.shape, 1)
    logf = jnp.where(lane < heads, _log_sigmoid(fr), 0.0)
    c = _dot_exact_lhs01(tri_ref[...], logf) + carry_ref[...]
    carry_ref[...] = c[c.shape[0] - 1:, :]
    parts = _bias_parts(c * LOG2E, heads)
    hd = D // heads
    k_ref[0] = k
    v_ref[0] = v
    vb_ref[0] = v.astype(BF16)
    lf_ref[0] = logf[:, :heads]
    c_ref[0] = c[:, :heads]
    kn_ref[0, 0] = _max_sq_key_norm(k, gn_ref[...])
    qs_ref[0] = _slotted(q, _dot(parts, pq_ref[...]), hd)
    ks_ref[0] = _slotted(k, _dot(parts, pk_ref[...]), hd)


def _fox_proj(x, w_cat, bf_pad, c0, *, heads):
    B, L, D = x.shape
    T = _pick_tile(L, 256)
    tri = jnp.asarray(np.tril(np.ones((T, T), np.float32)), BF16)
    pq, pk = _placement_matrices(heads, D // heads)
    gn = _key_norm_matrix(heads, D // heads)
    S = heads * HEAD_SLOT
    row = lambda n: pl.BlockSpec((1, T, n), lambda bi, i: (bi, i, 0))
    return pl.pallas_call(
        functools.partial(_fox_proj_body, heads=heads),
        grid=(B, L // T),
        in_specs=[row(D), _resident(w_cat.shape), _resident((1, LANES)),
                  pl.BlockSpec((1, 1, LANES), lambda bi, i: (bi, 0, 0)),
                  _resident((T, T)), _resident((LANES, S)), _resident((LANES, S)), _resident((D, LANES))],
        out_specs=[row(D), row(D), row(heads), row(heads),
                   pl.BlockSpec((1, 1, 1, LANES), lambda bi, i: (bi, i, 0, 0)),
                   row(S), row(S), row(D)],
        out_shape=[jax.ShapeDtypeStruct((B, L, D), F32), jax.ShapeDtypeStruct((B, L, D), F32),
                   jax.ShapeDtypeStruct((B, L, heads), F32), jax.ShapeDtypeStruct((B, L, heads), F32),
                   jax.ShapeDtypeStruct((B, L // T, 1, LANES), F32),
                   jax.ShapeDtypeStruct((B, L, S), BF16), jax.ShapeDtypeStruct((B, L, S), BF16),
                   jax.ShapeDtypeStruct((B, L, D), BF16)],
        scratch_shapes=[pltpu.VMEM((1, LANES), F32)],
        compiler_params=_compiler_params(("parallel", "arbitrary")),
        name="fox_proj",
    )(x, w_cat, bf_pad, c0, tri, pq, pk, gn)


def _fox_cache_body(k_ref, v_ref, lf_ref, tri_ref, pk_ref, gn_ref,
                    ks_ref, vb_ref, c_ref, kn_ref, cend_ref, carry_ref, *, heads):
    l = pl.program_id(1)

    @pl.when(l == 0)
    def _():
        carry_ref[...] = jnp.zeros(carry_ref.shape, F32)

    D = k_ref.shape[2]
    k = k_ref[0]
    c = _dot_exact_lhs01(tri_ref[...], lf_ref[0]) + carry_ref[...]
    last = c[c.shape[0] - 1:, :]
    carry_ref[...] = last
    cend_ref[0] = last
    c_ref[0] = c[:, :heads]
    kn_ref[0, 0] = _max_sq_key_norm(k, gn_ref[...])
    parts = _bias_parts(c * LOG2E, heads)
    ks_ref[0] = _slotted(k, _dot(parts, pk_ref[...]), D // heads)
    vb_ref[0] = v_ref[0].astype(BF16)


def _fox_cache(k, v, lf_pad, *, heads):
    B, P, D = k.shape
    T = _pick_tile(P, 256)
    tri = jnp.asarray(np.tril(np.ones((T, T), np.float32)), BF16)
    _, pk = _placement_matrices(heads, D // heads)
    gn = _key_norm_matrix(heads, D // heads)
    S = heads * HEAD_SLOT
    row = lambda n: pl.BlockSpec((1, T, n), lambda bi, i: (bi, i, 0))
    return pl.pallas_call(
        functools.partial(_fox_cache_body, heads=heads),
        grid=(B, P // T),
        in_specs=[row(D), row(D), row(LANES), _resident((T, T)), _resident((LANES, S)), _resident((D, LANES))],
        out_specs=[row(S), row(D), row(heads),
                   pl.BlockSpec((1, 1, 1, LANES), lambda bi, i: (bi, i, 0, 0)),
                   pl.BlockSpec((1, 1, LANES), lambda bi, i: (bi, 0, 0))],
        out_shape=[jax.ShapeDtypeStruct((B, P, S), BF16), jax.ShapeDtypeStruct((B, P, D), BF16),
                   jax.ShapeDtypeStruct((B, P, heads), F32),
                   jax.ShapeDtypeStruct((B, P // T, 1, LANES), F32),
                   jax.ShapeDtypeStruct((B, 1, LANES), F32)],
        scratch_shapes=[pltpu.VMEM((1, LANES), F32)],
        compiler_params=_compiler_params(("parallel", "arbitrary")),
        name="fox_cache",
    )(k, v, lf_pad, tri, pk, gn)


def _fox_attn_body(q_ref, k_ref, v_ref, cend_ref, kn_ref, o_ref,
                   sd_ref, sa_ref, md_ref, ma_ref, p_ref, m_ref, l_ref, acc_ref,
                   *, tq, tkd, tko, q_off, head_dim, strip):
    i = pl.program_id(2)
    q_lo = q_off + i * tq
    d0 = (q_lo // tkd) * tkd
    n_off = d0 // tko
    lane = lax.broadcasted_iota(jnp.int32, (tq, LANES), 1)
    blk = lax.broadcasted_iota(jnp.int32, (1, LANES), 1)
    n_head = LANES // head_dim
    qs = [q_ref[0, :, e * HEAD_SLOT:(e + 1) * HEAD_SLOT] for e in range(n_head)]

    def scores(start, width, s_ref, mx_ref, masked):
        for e in range(n_head):
            k = k_ref[0, pl.ds(start, width), e * HEAD_SLOT:(e + 1) * HEAD_SLOT]
            s = _dot_nt(qs[e], k)
            if masked:
                qpos = q_lo + lax.broadcasted_iota(jnp.int32, (tq, 1), 0)
                kpos = start + lax.broadcasted_iota(jnp.int32, (1, width), 1)
                s = jnp.where(kpos <= qpos, s, MASK_VALUE)
            s_ref[e] = s
            mx_ref[e] = jnp.broadcast_to(jnp.max(s, -1, keepdims=True), (tq, LANES))

    def absorb(start, width, s_ref, mx_ref):
        v = v_ref[0, pl.ds(start, width), :]
        for e in range(n_head):
            m = m_ref[e]
            m_new = jnp.maximum(m, mx_ref[e])
            m_ref[e] = m_new
            a = jnp.exp2(m - m_new)
            for r in range(tq // strip):
                rows = slice(r * strip, (r + 1) * strip)
                part = jnp.zeros((strip, LANES), F32)
                for t in range(width // LANES):
                    cols = slice(t * LANES, (t + 1) * LANES)
                    p = jnp.exp2(s_ref[e, rows, cols] - m_new[rows])
                    p_ref[e, rows, cols] = p.astype(BF16)
                    part = part + p
                l_ref[e, rows, :] = a[rows] * l_ref[e, rows, :] + part
            acc_ref[e] = a * acc_ref[e] + _dot(p_ref[e, :, :width], v)

    def absorb_fixed(start, width):
        v = v_ref[0, pl.ds(start, width), :]
        for e in range(n_head):
            s = _dot_nt(qs[e], k_ref[0, pl.ds(start, width), e * HEAD_SLOT:(e + 1) * HEAD_SLOT])
            m = m_ref[e]
            for r in range(tq // strip):
                rows = slice(r * strip, (r + 1) * strip)
                part = l_ref[e, rows, :]
                for t in range(width // LANES):
                    cols = slice(t * LANES, (t + 1) * LANES)
                    p = jnp.exp2(s[rows, cols] - m[rows])
                    p_ref[e, rows, cols] = p.astype(BF16)
                    part = part + p
                l_ref[e, rows, :] = part
            acc_ref[e] = acc_ref[e] + _dot(p_ref[e, :, :width], v)

    def diagonal_block():
        m_ref[...] = jnp.full(m_ref.shape, MASK_VALUE, F32)
        l_ref[...] = jnp.zeros(l_ref.shape, F32)
        acc_ref[...] = jnp.zeros(acc_ref.shape, F32)
        scores(pl.multiple_of(d0, tkd), tkd, sd_ref, md_ref, True)
        absorb(pl.multiple_of(d0, tkd), tkd, sd_ref, md_ref)

    diagonal_block()
    n_live = jnp.int32(0)
    for e in range(n_head):
        qf = qs[e].astype(F32)
        q_norm = jnp.sqrt(jnp.sum(jnp.where(lane < head_dim, qf * qf, 0.0), -1, keepdims=True))
        c_q = jnp.sum(jnp.where((lane >= head_dim) & (lane < head_dim + 3), qf, 0.0), -1, keepdims=True)
        k_norm = jnp.sqrt(kn_ref[0, 0, e:e + 1, 0:1]) * NORM_SLACK
        reach = jnp.max(q_norm * k_norm + c_q - m_ref[e, :, 0:1], 0, keepdims=True)
        live = (blk < n_off) & (cend_ref[0, 0, e:e + 1, :] * LOG2E <= reach + EXP_UNDERFLOW * LOG2E)
        n_live = jnp.maximum(n_live, jnp.sum(jnp.where(live, 1, 0)))

    def two_blocks(u, _):
        j0 = n_off - 1 - 2 * u
        absorb_fixed(pl.multiple_of(j0 * tko, tko), tko)
        absorb_fixed(pl.multiple_of((j0 - 1) * tko, tko), tko)
        return 0

    lax.fori_loop(0, (n_live + 1) // 2, two_blocks, 0)

    n_bad = jnp.float32(0.0)
    for e in range(n_head):
        fine = jnp.isfinite(l_ref[e]) & jnp.isfinite(acc_ref[e])
        n_bad = n_bad + jnp.sum(jnp.where(fine, 0.0, 1.0))

    @pl.when(n_bad > 0.0)
    def _():
        diagonal_block()

        def one_block(t, _):
            start = pl.multiple_of((n_off - 1 - t) * tko, tko)
            scores(start, tko, sa_ref, ma_ref, False)
            absorb(start, tko, sa_ref, ma_ref)
            return 0

        lax.fori_loop(0, n_live, one_block, 0)

    outs = [acc_ref[e] / jnp.sum(l_ref[e], -1, keepdims=True) for e in range(n_head)]
    o_ref[0] = jnp.where(lane < head_dim, outs[0], outs[1]).astype(BF16)


def _fox_attn(qs, ks, vb, c_keys, kn2, *, q_off, head_dim):
    B, Lq, S = qs.shape
    Lk = ks.shape[1]
    D = vb.shape[2]
    heads = D // head_dim
    tq = _pick_tile(Lq, 512)
    tkd = Lk if Lk <= 2048 else tq
    tko = _pick_tile(tkd, 256)
    n_blk = Lk // tko
    assert tkd % tq == 0 and q_off % tq == 0 and q_off + Lq <= Lk and Lk % tkd == 0
    assert Lk == tkd or tkd % (2 * tko) == 0
    assert LANES // head_dim == 2 and n_blk <= LANES
    cend = jnp.swapaxes(c_keys[:, tko - 1::tko, :], 1, 2)
    cend = jnp.pad(cend, ((0, 0), (0, 0), (0, LANES - n_blk))).reshape(B, heads // 2, 2, LANES)
    knb = jnp.broadcast_to(kn2.reshape(B, heads // 2, 2, 1), (B, heads // 2, 2, LANES))
    pair = lambda: pl.BlockSpec((1, 1, 2, LANES), lambda bi, p, i: (bi, p, 0, 0))
    stat = lambda: pltpu.VMEM((2, tq, LANES), F32)
    return pl.pallas_call(
        functools.partial(_fox_attn_body, tq=tq, tkd=tkd, tko=tko, q_off=q_off, head_dim=head_dim,
                          strip=min(tq, 32)),
        grid=(B, D // LANES, Lq // tq),
        in_specs=[pl.BlockSpec((1, tq, 2 * HEAD_SLOT), lambda bi, p, i: (bi, i, p)),
                  pl.BlockSpec((1, Lk, 2 * HEAD_SLOT), lambda bi, p, i: (bi, 0, p)),
                  pl.BlockSpec((1, Lk, LANES), lambda bi, p, i: (bi, 0, p)),
                  pair(), pair()],
        out_specs=pl.BlockSpec((1, tq, LANES), lambda bi, p, i: (bi, i, p)),
        out_shape=jax.ShapeDtypeStruct((B, Lq, D), BF16),
        scratch_shapes=[pltpu.VMEM((2, tq, tkd), F32),
                        pltpu.VMEM((2, tq, tko), F32),
                        stat(), stat(),
                        pltpu.VMEM((2, tq, tkd), BF16),
                        stat(),
                        stat(),
                        stat()],
        compiler_params=_compiler_params(("parallel", "parallel", "arbitrary")),
        name="fox_attn",
    )(qs, ks, vb, cend, knb)


def _fox_weights(w_in, b_f, heads):
    D = w_in.shape[0]
    scale = (D // heads) ** -0.5 * LOG2E
    wf = jnp.pad(w_in[:, 3 * D:], ((0, 0), (0, LANES - heads)))
    w_cat = jnp.concatenate([w_in[:, :D] * scale, w_in[:, D:3 * D], wf], axis=1).astype(BF16)
    bf_pad = jnp.pad(b_f, (0, LANES - heads)).reshape(1, LANES)
    return w_cat, bf_pad


def _sigmoid(x):
    return 1.0 / (1.0 + jnp.exp(-x))


def _hgrn_body(x_ref, win_ref, wo_ref, lbraw_ref, ng_ref, g_ref, b_ref, s0_ref, tri_ref,
               y_ref, sfin_ref, st_ref, qg_ref, kt_ref, kh_ref, qi_ref, vb_ref, eb_ref, o_ref,
               *, alpha, layer, T, heads):
    l = pl.program_id(1)
    D = x_ref.shape[2]
    hd = D // heads
    n_sub = CHUNK // SUBLANES
    n_chunk = T // CHUNK

    @pl.when(l == 0)
    def _():
        for h in range(heads):
            st_ref[h] = s0_ref[0, h].T

    lbr = lbraw_ref[...]
    e = jnp.exp(lbr - jnp.max(lbr, 0, keepdims=True))
    pr = e / jnp.sum(e, 0, keepdims=True)
    lb = jnp.zeros((1, D), F32)
    for r in range(1, layer + 1):
        lb = lb + pr[r:r + 1, :]

    x = x_ref[0]
    hp = _dot(x.astype(BF16), win_ref[...])
    qr, fr, v, gate = hp[:, :D], hp[:, D:2 * D], hp[:, 2 * D:3 * D], hp[:, 3 * D:]
    q = qr * _sigmoid(qr)
    logf = jnp.log(lb + (1.0 - lb) * _sigmoid(fr))
    kk = (1.0 - lb) * _sigmoid(-fr)
    b = _dot_exact_lhs01(tri_ref[...], logf, pieces=2)
    bc = b.reshape(n_chunk, CHUNK, D)
    b_last = jnp.broadcast_to(bc[:, CHUNK - 1:, :], bc.shape).reshape(T, D)

    kh_ref[...] = (kk * jnp.exp(b_last - b)).astype(BF16)
    qi_ref[...] = (q * jnp.exp(b)).astype(BF16)
    vb_ref[...] = v.astype(BF16)
    eb_ref[...] = jnp.exp(b_last)
    mild = jnp.min(b) >= -DECAY_LIMIT

    @pl.when(mild)
    def _():
        kt_ref[...] = (kk * jnp.exp(-b)).astype(BF16)
        row = lax.broadcasted_iota(jnp.int32, (T, T), 0)
        col = lax.broadcasted_iota(jnp.int32, (T, T), 1)
        shift = CHUNK.bit_length() - 1
        keep = (lax.shift_right_logical(row, shift) == lax.shift_right_logical(col, shift)) & (col <= row)
        for h in range(heads):
            sl = slice(h * hd, (h + 1) * hd)
            a = jnp.where(keep, _dot_nt(qi_ref[:, sl], kt_ref[:, sl]), 0.0)
            o_h = _dot(a.astype(BF16), vb_ref[:, sl])
            st = st_ref[h]
            inter = []
            for c in range(n_chunk):
                rc = slice(c * CHUNK, (c + 1) * CHUNK)
                inter.append(_dot_nt(qi_ref[rc, sl], st.astype(BF16)))
                st = st * eb_ref[c * CHUNK:c * CHUNK + 1, sl] + _dot_tn(vb_ref[rc, sl], kh_ref[rc, sl])
            st_ref[h] = st
            o_ref[:, sl] = o_h + jnp.concatenate(inter, 0)

    @pl.when(jnp.logical_not(mild))
    def _():
        shape3 = (T // SUBLANES, SUBLANES, D)
        b3 = b.reshape(shape3)
        lf3 = logf.reshape(shape3)
        b_end = jnp.broadcast_to(b3[:, SUBLANES - 1:, :], shape3).reshape(T, D)
        b_start = jnp.broadcast_to(b3[:, :1, :] - lf3[:, :1, :], shape3).reshape(T, D)
        qt = q * jnp.exp(jnp.minimum(b - b_start, 0.0))
        dm = jnp.exp(jnp.minimum(b_end - b_start, 0.0))
        qg_ref[0] = qt.astype(BF16)
        for gp in range(1, n_sub - 1):
            shifted = jnp.concatenate([jnp.ones((gp * SUBLANES, D), F32), dm[:T - gp * SUBLANES, :]], 0)
            qt = qt * shifted
            qg_ref[gp] = qt.astype(BF16)
        kt_ref[...] = (kk * jnp.exp(b_end - b)).astype(BF16)

        q3 = q.reshape(shape3)
        k3 = kk.reshape(shape3)
        v3 = v.reshape(shape3)
        sub = lax.broadcasted_iota(jnp.int32, (T // SUBLANES, SUBLANES, 1), 1)
        od = [jnp.zeros((T // SUBLANES, SUBLANES, hd), F32) for _ in range(heads)]
        for d in range(SUBLANES):
            kr = k3 if d == 0 else pltpu.roll(k3, d, axis=1)
            br = b3 if d == 0 else pltpu.roll(b3, d, axis=1)
            vr = v3 if d == 0 else pltpu.roll(v3, d, axis=1)
            w = q3 * kr * jnp.exp(jnp.minimum(b3 - br, 0.0))
            for h in range(heads):
                sl = slice(h * hd, (h + 1) * hd)
                a = jnp.sum(w[:, :, sl], -1, keepdims=True)
                od[h] = od[h] + jnp.where(sub >= d, a, 0.0) * vr[:, :, sl]
        for h in range(heads):
            o_ref[:, h * hd:(h + 1) * hd] = od[h].reshape(T, hd)

        rsub = lax.shift_right_logical(lax.broadcasted_iota(jnp.int32, (CHUNK, CHUNK), 0), 3)
        csub = lax.shift_right_logical(lax.broadcasted_iota(jnp.int32, (CHUNK, CHUNK), 1), 3)
        gap = rsub - csub

        def chunk_step(c, _):
            rows = pl.ds(pl.multiple_of(c * CHUNK, CHUNK), CHUNK)
            for h in range(heads):
                sl = slice(h * hd, (h + 1) * hd)
                kt = kt_ref[rows, sl]
                vb = vb_ref[rows, sl]
                a_off = jnp.zeros((CHUNK, CHUNK), F32)
                for gp in range(n_sub - 1):
                    a_g = _dot_nt(qg_ref[gp, rows, sl], kt)
                    a_off = a_off + jnp.where(gap == gp + 1, a_g, 0.0)
                st = st_ref[h]
                o_h = _dot(a_off.astype(BF16), vb) + _dot_nt(qi_ref[rows, sl], st.astype(BF16))
                o_ref[rows, sl] = o_ref[rows, sl] + o_h
                u_t = _dot_tn(vb, kh_ref[rows, sl])
                st_ref[h] = st * eb_ref[pl.ds(c * CHUNK, 1), sl] + u_t
            return 0

        lax.fori_loop(0, n_chunk, chunk_step, 0)

    @pl.when(l == pl.num_programs(1) - 1)
    def _():
        for h in range(heads):
            sfin_ref[0, h] = st_ref[h].T

    outs = []
    for h in range(heads):
        oh = o_ref[:, h * hd:(h + 1) * hd]
        outs.append(oh * lax.rsqrt(jnp.mean(oh * oh, -1, keepdims=True) + RMS_EPS))
    o = jnp.concatenate(outs, -1) * ng_ref[...] * (gate * _sigmoid(gate))
    y = _dot(o.astype(BF16), wo_ref[...])
    y_ref[0] = _layer_norm(alpha * x + y, g_ref[...], b_ref[...])


def _hgrn_layer(x, s0, w_in, w_o, lb_raw, norm_g, g, b, *, alpha, layer):
    B, L, D = x.shape
    heads = D // HGRN_HEAD_DIM
    hd = HGRN_HEAD_DIM
    T = _pick_tile(L, 256)
    assert T % CHUNK == 0
    tri = jnp.asarray(np.kron(np.eye(T // CHUNK, dtype=np.float32), np.tril(np.ones((CHUNK, CHUNK), np.float32))), BF16)
    n_gap = CHUNK // SUBLANES - 1
    return pl.pallas_call(
        functools.partial(_hgrn_body, alpha=alpha, layer=layer, T=T, heads=heads),
        grid=(B, L // T),
        in_specs=[pl.BlockSpec((1, T, D), lambda bi, i: (bi, i, 0)),
                  _resident(w_in.shape), _resident(w_o.shape), _resident(lb_raw.shape),
                  _resident((1, D)), _resident((1, D)), _resident((1, D)),
                  pl.BlockSpec((1, heads, hd, hd), lambda bi, i: (bi, 0, 0, 0)),
                  _resident((T, T))],
        out_specs=[pl.BlockSpec((1, T, D), lambda bi, i: (bi, i, 0)),
                   pl.BlockSpec((1, heads, hd, hd), lambda bi, i: (bi, 0, 0, 0))],
        out_shape=[jax.ShapeDtypeStruct((B, L, D), F32), jax.ShapeDtypeStruct((B, heads, hd, hd), F32)],
        scratch_shapes=[pltpu.VMEM((heads, hd, hd), F32),
                        pltpu.VMEM((n_gap, T, D), BF16),
                        pltpu.VMEM((T, D), BF16),
                        pltpu.VMEM((T, D), BF16),
                        pltpu.VMEM((T, D), BF16),
                        pltpu.VMEM((T, D), BF16),
                        pltpu.VMEM((T, D), F32),
                        pltpu.VMEM((T, D), F32)],
        compiler_params=_compiler_params(("parallel", "arbitrary")),
        name="hgrn_layer",
    )(x, w_in, w_o, lb_raw, norm_g, g, b, s0, tri)


def kernel(x_prompt, x_sample, cache_fox_k, cache_fox_v, cache_fox_logf, state_pool, state_hgrn, pool_w, pool_scale, fox_w_in, fox_b_f, fox_w_o, hgrn_w_in, hgrn_lb_raw, hgrn_norm_g, hgrn_w_o, ln_mix_g, ln_mix_b, w_up, w_down, ln_ffn_g, ln_ffn_b):
    depth = w_up.shape[0]
    alpha = float((2 * depth) ** 0.25)
    B, L, D = x_prompt.shape
    Bs, Ls, _ = x_sample.shape
    past = cache_fox_k.shape[2]
    heads = FOX_HEADS
    hd = D // heads
    pool_state = state_pool.shape[2]
    row = lambda a: a.reshape(1, D)

    xp, xs = x_prompt, x_sample
    pool_p, pool_s = [], []
    fk_p, fv_p, fl_p, fk_s, fv_s, fl_s = [], [], [], [], [], []
    hg_p, hg_s = [], []
    for i in range(depth):
        kind, j = i % N_MIXERS, i // N_MIXERS
        g_mix, b_mix = row(ln_mix_g[i]), row(ln_mix_b[i])
        mlp = (w_up[i].astype(BF16), w_down[i].astype(BF16), row(ln_ffn_g[i]), row(ln_ffn_b[i]))
        if kind == 0:
            w = pool_w[j].astype(BF16)
            scale = row(pool_scale[j])
            hist_p = jnp.zeros((B, POOL_HIST, D), F32)
            hist_s = jnp.pad(state_pool[j], ((0, 0), (POOL_HIST - pool_state, 0), (0, 0)))
            pool_p.append(xp[:, L - pool_state:])
            pool_s.append(xs[:, Ls - pool_state:])
            xp = _pool_mlp(xp, hist_p, w, scale, g_mix, b_mix, mlp, alpha=alpha, pos0=0)
            xs = _pool_mlp(xs, hist_s, w, scale, g_mix, b_mix, mlp, alpha=alpha, pos0=past)
        elif kind == 1:
            w_cat, bf_pad = _fox_weights(fox_w_in[j], fox_b_f[j], heads)
            w_o = fox_w_o[j].astype(BF16)
            k, v, lf, c, kn, qs, ks, vb = _fox_proj(xp, w_cat, bf_pad, jnp.zeros((B, 1, LANES), F32), heads=heads)
            o = _fox_attn(qs, ks, vb, c, jnp.max(kn, (1, 2))[:, :heads], q_off=0, head_dim=hd)
            xp = _proj_mlp(o.reshape(B * L, D), w_o, xp.reshape(B * L, D), g_mix, b_mix, mlp,
                           alpha=alpha).reshape(B, L, D)
            fk_p.append(k.reshape(B, L, heads, hd))
            fv_p.append(v.reshape(B, L, heads, hd))
            fl_p.append(lf)
            lf_pad = jnp.pad(cache_fox_logf[j], ((0, 0), (0, 0), (0, LANES - heads)))
            ks_c, vb_c, c_c, kn_c, c_end = _fox_cache(cache_fox_k[j].reshape(Bs, past, D),
                                                      cache_fox_v[j].reshape(Bs, past, D), lf_pad, heads=heads)
            k, v, lf, c, kn, qs, ks, vb = _fox_proj(xs, w_cat, bf_pad, c_end, heads=heads)
            tail = (-(past + Ls)) % LANES
            grow = ((0, 0), (0, tail), (0, 0))
            ks_all = jnp.pad(jnp.concatenate([ks_c, ks], 1), grow)
            vb_all = jnp.pad(jnp.concatenate([vb_c, vb], 1), grow)
            c_all = jnp.pad(jnp.concatenate([c_c, c], 1), grow, mode="edge")
            kn_all = jnp.maximum(jnp.max(kn_c, (1, 2)), jnp.max(kn, (1, 2)))[:, :heads]
            o = _fox_attn(qs, ks_all, vb_all, c_all, kn_all, q_off=past, head_dim=hd)
            xs = _proj_mlp(o.reshape(Bs * Ls, D), w_o, xs.reshape(Bs * Ls, D), g_mix, b_mix, mlp,
                           alpha=alpha).reshape(Bs, Ls, D)
            fk_s.append(k.reshape(Bs, Ls, heads, hd))
            fv_s.append(v.reshape(Bs, Ls, heads, hd))
            fl_s.append(lf)
        else:
            w_in = hgrn_w_in[j].astype(BF16)
            w_o = hgrn_w_o[j].astype(BF16)
            ng = row(hgrn_norm_g[j])
            s0 = jnp.zeros((B,) + state_hgrn.shape[2:], F32)
            xp, sp = _hgrn_layer(xp, s0, w_in, w_o, hgrn_lb_raw, ng, g_mix, b_mix, alpha=alpha, layer=i)
            xs, ss = _hgrn_layer(xs, state_hgrn[j], w_in, w_o, hgrn_lb_raw, ng, g_mix, b_mix, alpha=alpha, layer=i)
            hg_p.append(sp)
            hg_s.append(ss)
            xp = _mlp_ln(xp.reshape(B * L, D), mlp, alpha=alpha).reshape(B, L, D)
            xs = _mlp_ln(xs.reshape(Bs * Ls, D), mlp, alpha=alpha).reshape(Bs, Ls, D)
    return (xp, xs, jnp.stack(pool_p), jnp.stack(pool_s),
            jnp.stack(fk_p), jnp.stack(fv_p), jnp.stack(fl_p),
            jnp.stack(fk_s), jnp.stack(fv_s), jnp.stack(fl_s),
            jnp.stack(hg_p), jnp.stack(hg_s))
```

```python
import functools

import jax
import jax.numpy as jnp
import numpy as np
from jax import lax
from jax.experimental import pallas as pl
from jax.experimental.pallas import tpu as pltpu

F32 = jnp.float32
BF16 = jnp.bfloat16

CHUNK = 64
N_MIXERS = 3
POOL_WINDOWS = (2, 4, 8, 16)
POOL_HIST = 16
FOX_HEADS = 16
HGRN_HEAD_DIM = 128
LN_EPS = 1e-5
RMS_EPS = 1e-6

LANES = 128
SUBLANES = 8
VMEM_LIMIT_BYTES = 56 * 1024 * 1024

HEAD_SLOT = LANES
MASK_VALUE = -1e30
LOG2E = float(np.log2(np.e))
EXP_UNDERFLOW = 90.0
NORM_SLACK = 1.02
DECAY_LIMIT = 60.0


def _compiler_params(semantics):
    return pltpu.CompilerParams(dimension_semantics=semantics, vmem_limit_bytes=VMEM_LIMIT_BYTES)


def _resident(shape):
    return pl.BlockSpec(shape, lambda *_: (0,) * len(shape), pipeline_mode=pl.Buffered(1))


def _pick_tile(n, pref):
    t = min(n, pref)
    while n % t:
        t //= 2
    return t


def _layer_norm(z, g, b):
    mu = jnp.mean(z, -1, keepdims=True)
    zc = z - mu
    var = jnp.mean(zc * zc, -1, keepdims=True)
    return zc * lax.rsqrt(var + LN_EPS) * g + b


def _split3(x):
    hi = x.astype(BF16)
    r1 = x - hi.astype(F32)
    mid = r1.astype(BF16)
    lo = (r1 - mid.astype(F32)).astype(BF16)
    return hi, mid, lo


def _dot(a, b):
    return jnp.dot(a, b, preferred_element_type=F32)


def _dot_nt(a, b):
    return lax.dot_general(a, b, (((1,), (1,)), ((), ())), preferred_element_type=F32)


def _dot_tn(a, b):
    return lax.dot_general(a, b, (((0,), (0,)), ((), ())), preferred_element_type=F32)


def _dot_exact_lhs01(m01, x, pieces=3):
    hi, mid, lo = _split3(x)
    y = _dot(m01, hi) + _dot(m01, mid)
    return y + _dot(m01, lo) if pieces == 3 else y


FF_CHUNK = 512


def _mlp_ln_value(x, wup_ref, wdn_ref, g, b, alpha):
    xb = x.astype(BF16)
    d_ff = wup_ref.shape[1]
    ff_chunk = _pick_tile(d_ff, FF_CHUNK)
    acc = jnp.zeros(x.shape, F32)
    for c in range(d_ff // ff_chunk):
        sl = slice(c * ff_chunk, (c + 1) * ff_chunk)
        h = jnp.maximum(_dot(xb, wup_ref[:, sl]), 0.0)
        acc = acc + _dot((h * h).astype(BF16), wdn_ref[sl, :])
    return _layer_norm(alpha * x + acc, g, b)


def _mlp_specs(D, d_ff):
    return [_resident((D, d_ff)), _resident((d_ff, D)), _resident((1, D)), _resident((1, D))]


def _mlp_ln_body(x_ref, wup_ref, wdn_ref, g_ref, b_ref, o_ref, *, alpha):
    o_ref[...] = _mlp_ln_value(x_ref[...], wup_ref, wdn_ref, g_ref[...], b_ref[...], alpha)


def _mlp_ln(x2, mlp, *, alpha):
    R, D = x2.shape
    tm = _pick_tile(R, 512)
    return pl.pallas_call(
        functools.partial(_mlp_ln_body, alpha=alpha),
        grid=(R // tm,),
        in_specs=[pl.BlockSpec((tm, D), lambda i: (i, 0))] + _mlp_specs(D, mlp[0].shape[1]),
        out_specs=pl.BlockSpec((tm, D), lambda i: (i, 0)),
        out_shape=jax.ShapeDtypeStruct((R, D), F32),
        compiler_params=_compiler_params(("parallel",)),
        name="mlp_ln",
    )(x2, *mlp)


def _proj_mlp_body(a_ref, w_ref, x_ref, g_ref, b_ref, wup_ref, wdn_ref, g2_ref, b2_ref, o_ref, *, alpha):
    y = _dot(a_ref[...], w_ref[...])
    x1 = _layer_norm(alpha * x_ref[...] + y, g_ref[...], b_ref[...])
    o_ref[...] = _mlp_ln_value(x1, wup_ref, wdn_ref, g2_ref[...], b2_ref[...], alpha)


def _proj_mlp(a2, w, x2, g, b, mlp, *, alpha):
    R, D = x2.shape
    tm = _pick_tile(R, 512)
    return pl.pallas_call(
        functools.partial(_proj_mlp_body, alpha=alpha),
        grid=(R // tm,),
        in_specs=[pl.BlockSpec((tm, D), lambda i: (i, 0)),
                  _resident((D, D)),
                  pl.BlockSpec((tm, D), lambda i: (i, 0)),
                  _resident((1, D)), _resident((1, D))] + _mlp_specs(D, mlp[0].shape[1]),
        out_specs=pl.BlockSpec((tm, D), lambda i: (i, 0)),
        out_shape=jax.ShapeDtypeStruct((R, D), F32),
        compiler_params=_compiler_params(("parallel",)),
        name="proj_mlp",
    )(a2, w, x2, g, b, *mlp)


def _pool_mlp_body(x_ref, xprev_ref, hist_ref, w_ref, scale_ref, g_ref, b_ref, wup_ref, wdn_ref, g2_ref, b2_ref,
                   o_ref, xa_ref, x1_ref, *, alpha, pos0, tm, n_seq):
    i = pl.program_id(1)
    group = x_ref.shape[2] // len(POOL_WINDOWS)
    n_avail = lax.broadcasted_iota(jnp.int32, (tm, 1), 0) + (i * tm + pos0 + 1)
    for q in range(n_seq):
        x = x_ref[q]
        xa_ref[0:POOL_HIST, :] = jnp.where(i == 0, hist_ref[q], xprev_ref[q])
        xa_ref[POOL_HIST:, :] = x
        ys = []
        for gi, w in enumerate(POOL_WINDOWS):
            sl = slice(gi * group, (gi + 1) * group)
            s = xa_ref[:, sl]
            sh = 1
            while sh < w:
                s = s + pltpu.roll(s, sh, axis=0)
                sh *= 2
            cnt = jnp.minimum(n_avail, w).astype(F32)
            pooled = s[POOL_HIST:, :] / cnt - x[:, sl]
            ys.append(_dot(pooled.astype(BF16), w_ref[gi]))
        y = jnp.concatenate(ys, -1) * scale_ref[...]
        x1_ref[q * tm:(q + 1) * tm, :] = _layer_norm(alpha * x + y, g_ref[...], b_ref[...])
    out = _mlp_ln_value(x1_ref[...], wup_ref, wdn_ref, g2_ref[...], b2_ref[...], alpha)
    for q in range(n_seq):
        o_ref[q] = out[q * tm:(q + 1) * tm, :]


def _pool_mlp(x, hist, w, scale, g, b, mlp, *, alpha, pos0):
    B, L, D = x.shape
    n_g, group = w.shape[0], w.shape[1]
    tm = _pick_tile(L, 512)
    n_seq = _pick_tile(B, 512 // tm) if tm == L else 1
    assert tm % POOL_HIST == 0 and max(POOL_WINDOWS) <= POOL_HIST
    per = tm // POOL_HIST
    return pl.pallas_call(
        functools.partial(_pool_mlp_body, alpha=alpha, pos0=pos0, tm=tm, n_seq=n_seq),
        grid=(B // n_seq, L // tm),
        in_specs=[pl.BlockSpec((n_seq, tm, D), lambda bi, i: (bi, i, 0)),
                  pl.BlockSpec((n_seq, POOL_HIST, D), lambda bi, i: (bi, jnp.maximum(i * per - 1, 0), 0)),
                  pl.BlockSpec((n_seq, POOL_HIST, D), lambda bi, i: (bi, 0, 0)),
                  _resident((n_g, group, group)),
                  _resident((1, D)), _resident((1, D)), _resident((1, D))] + _mlp_specs(D, mlp[0].shape[1]),
        out_specs=pl.BlockSpec((n_seq, tm, D), lambda bi, i: (bi, i, 0)),
        out_shape=jax.ShapeDtypeStruct((B, L, D), F32),
        scratch_shapes=[pltpu.VMEM((tm + POOL_HIST, D), F32), pltpu.VMEM((n_seq * tm, D), F32)],
        compiler_params=_compiler_params(("parallel", "arbitrary")),
        name="pool_mlp",
    )(x, x, hist, w, scale, g, b, *mlp)


def _placement_matrices(heads, head_dim):
    assert 3 * heads < LANES and head_dim + 6 <= HEAD_SLOT
    pq = np.zeros((LANES, heads * HEAD_SLOT), np.float32)
    pk = np.zeros((LANES, heads * HEAD_SLOT), np.float32)
    one = 3 * heads
    for h in range(heads):
        base = h * HEAD_SLOT + head_dim
        for p in range(3):
            pq[p * heads + h, base + p] = 1.0
            pq[one, base + 3 + p] = 1.0
            pk[one, base + p] = 1.0
            pk[p * heads + h, base + 3 + p] = -1.0
    return jnp.asarray(pq, BF16), jnp.asarray(pk, BF16)


def _bias_parts(c, heads):
    hi, mid, lo = _split3(c)
    lane = lax.broadcasted_iota(jnp.int32, c.shape, 1)
    parts = (hi.astype(F32) + pltpu.roll(mid.astype(F32), heads, axis=1)
             + pltpu.roll(lo.astype(F32), 2 * heads, axis=1) + jnp.where(lane == 3 * heads, 1.0, 0.0))
    return parts.astype(BF16)


def _slotted(nat, aug, head_dim):
    T = nat.shape[0]
    per = LANES // head_dim
    assert per == 2
    lane = lax.broadcasted_iota(jnp.int32, (T, LANES), 1)
    low = lane < head_dim
    out = []
    for p in range(nat.shape[1] // LANES):
        pair = nat[:, p * LANES:(p + 1) * LANES]
        swapped = pltpu.roll(pair, head_dim, axis=1)
        out.append(jnp.where(low, pair, aug[:, (2 * p) * HEAD_SLOT:(2 * p + 1) * HEAD_SLOT]))
        out.append(jnp.where(low, swapped, aug[:, (2 * p + 1) * HEAD_SLOT:(2 * p + 2) * HEAD_SLOT]))
    return jnp.concatenate(out, -1).astype(BF16)


def _log_sigmoid(x):
    return jnp.minimum(x, 0.0) - jnp.log(1.0 + jnp.exp(-jnp.abs(x)))


def _key_norm_matrix(heads, head_dim):
    g = np.zeros((heads * head_dim, LANES), np.float32)
    for h in range(heads):
        g[h * head_dim:(h + 1) * head_dim, h] = 1.0
    return jnp.asarray(g, BF16)


def _max_sq_key_norm(k, gn):
    return jnp.max(_dot((k * k).astype(BF16), gn), 0, keepdims=True)


def _fox_proj_body(x_ref, w_ref, bf_ref, c0_ref, tri_ref, pq_ref, pk_ref, gn_ref,
                   k_ref, v_ref, lf_ref, c_ref, kn_ref, qs_ref, ks_ref, vb_ref, carry_ref, *, heads):
    l = pl.program_id(1)
    D = x_ref.shape[2]

    @pl.when(l == 0)
    def _():
        carry_ref[...] = c0_ref[0]

    h = _dot(x_ref[0].astype(BF16), w_ref[...])
    q, k, v = h[:, :D], h[:, D:2 * D], h[:, 2 * D:3 * D]
    fr = h[:, 3 * D:] + bf_ref[...]
    lane = lax.broadcasted_iota(jnp.int32, fr.shape, 1)
    logf = jnp.where(lane < heads, _log_sigmoid(fr), 0.0)
    c = _dot_exact_lhs01(tri_ref[...], logf) + carry_ref[...]
    carry_ref[...] = c[c.shape[0] - 1:, :]
    parts = _bias_parts(c * LOG2E, heads)
    hd = D // heads
    k_ref[0] = k
    v_ref[0] = v
    vb_ref[0] = v.astype(BF16)
    lf_ref[0] = logf[:, :heads]
    c_ref[0] = c[:, :heads]
    kn_ref[0, 0] = _max_sq_key_norm(k, gn_ref[...])
    qs_ref[0] = _slotted(q, _dot(parts, pq_ref[...]), hd)
    ks_ref[0] = _slotted(k, _dot(parts, pk_ref[...]), hd)


def _fox_proj(x, w_cat, bf_pad, c0, *, heads):
    B, L, D = x.shape
    T = _pick_tile(L, 256)
    tri = jnp.asarray(np.tril(np.ones((T, T), np.float32)), BF16)
    pq, pk = _placement_matrices(heads, D // heads)
    gn = _key_norm_matrix(heads, D // heads)
    S = heads * HEAD_SLOT
    row = lambda n: pl.BlockSpec((1, T, n), lambda bi, i: (bi, i, 0))
    return pl.pallas_call(
        functools.partial(_fox_proj_body, heads=heads),
        grid=(B, L // T),
        in_specs=[row(D), _resident(w_cat.shape), _resident((1, LANES)),
                  pl.BlockSpec((1, 1, LANES), lambda bi, i: (bi, 0, 0)),
                  _resident((T, T)), _resident((LANES, S)), _resident((LANES, S)), _resident((D, LANES))],
        out_specs=[row(D), row(D), row(heads), row(heads),
                   pl.BlockSpec((1, 1, 1, LANES), lambda bi, i: (bi, i, 0, 0)),
                   row(S), row(S), row(D)],
        out_shape=[jax.ShapeDtypeStruct((B, L, D), F32), jax.ShapeDtypeStruct((B, L, D), F32),
                   jax.ShapeDtypeStruct((B, L, heads), F32), jax.ShapeDtypeStruct((B, L, heads), F32),
                   jax.ShapeDtypeStruct((B, L // T, 1, LANES), F32),
                   jax.ShapeDtypeStruct((B, L, S), BF16), jax.ShapeDtypeStruct((B, L, S), BF16),
                   jax.ShapeDtypeStruct((B, L, D), BF16)],
        scratch_shapes=[pltpu.VMEM((1, LANES), F32)],
        compiler_params=_compiler_params(("parallel", "arbitrary")),
        name="fox_proj",
    )(x, w_cat, bf_pad, c0, tri, pq, pk, gn)


def _fox_cache_body(k_ref, v_ref, lf_ref, tri_ref, pk_ref, gn_ref,
                    ks_ref, vb_ref, c_ref, kn_ref, cend_ref, carry_ref, *, heads):
    l = pl.program_id(1)

    @pl.when(l == 0)
    def _():
        carry_ref[...] = jnp.zeros(carry_ref.shape, F32)

    hd = k_ref.shape[3]
    D = heads * hd
    k = jnp.concatenate([k_ref[0, :, h, :] for h in range(heads)], -1)
    v = jnp.concatenate([v_ref[0, :, h, :] for h in range(heads)], -1)
    c = _dot_exact_lhs01(tri_ref[...], lf_ref[0]) + carry_ref[...]
    last = c[c.shape[0] - 1:, :]
    carry_ref[...] = last
    cend_ref[0] = last
    c_ref[0] = c[:, :heads]
    kn_ref[0, 0] = _max_sq_key_norm(k, gn_ref[...])
    parts = _bias_parts(c * LOG2E, heads)
    ks_ref[0] = _slotted(k, _dot(parts, pk_ref[...]), hd)
    vb_ref[0] = v.astype(BF16)


def _fox_cache(k, v, lf_pad, *, heads):
    B, P, _, hd = k.shape
    D = heads * hd
    T = _pick_tile(P, 256)
    tri = jnp.asarray(np.tril(np.ones((T, T), np.float32)), BF16)
    _, pk = _placement_matrices(heads, D // heads)
    gn = _key_norm_matrix(heads, D // heads)
    S = heads * HEAD_SLOT
    row = lambda n: pl.BlockSpec((1, T, n), lambda bi, i: (bi, i, 0))
    kv_in = pl.BlockSpec((1, T, heads, hd), lambda bi, i: (bi, i, 0, 0))
    return pl.pallas_call(
        functools.partial(_fox_cache_body, heads=heads),
        grid=(B, P // T),
        in_specs=[kv_in, kv_in, row(LANES), _resident((T, T)), _resident((LANES, S)), _resident((D, LANES))],
        out_specs=[row(S), row(D), row(heads),
                   pl.BlockSpec((1, 1, 1, LANES), lambda bi, i: (bi, i, 0, 0)),
                   pl.BlockSpec((1, 1, LANES), lambda bi, i: (bi, 0, 0))],
        out_shape=[jax.ShapeDtypeStruct((B, P, S), BF16), jax.ShapeDtypeStruct((B, P, D), BF16),
                   jax.ShapeDtypeStruct((B, P, heads), F32),
                   jax.ShapeDtypeStruct((B, P // T, 1, LANES), F32),
                   jax.ShapeDtypeStruct((B, 1, LANES), F32)],
        scratch_shapes=[pltpu.VMEM((1, LANES), F32)],
        compiler_params=_compiler_params(("parallel", "arbitrary")),
        name="fox_cache",
    )(k, v, lf_pad, tri, pk, gn)


def _fox_attn_body(q_ref, k_ref, v_ref, cend_ref, kn_ref, o_ref, s_ref, p_ref, m_ref, l_ref, acc_ref,
                   *, tq, tk, q_off, head_dim, strip):
    i = pl.program_id(2)
    q_lo = q_off + i * tq
    j_diag = (q_lo + tq - 1) // tk
    lane = lax.broadcasted_iota(jnp.int32, (tq, LANES), 1)
    blk = lax.broadcasted_iota(jnp.int32, (1, LANES), 1)
    n_head = LANES // head_dim
    qs = [q_ref[0, :, e * HEAD_SLOT:(e + 1) * HEAD_SLOT] for e in range(n_head)]

    def step(j, masked):
        start = pl.multiple_of(j * tk, tk)
        v = v_ref[0, pl.ds(start, tk), :]
        m_news = []
        for e in range(n_head):
            k = k_ref[0, pl.ds(start, tk), e * HEAD_SLOT:(e + 1) * HEAD_SLOT]
            s = _dot_nt(qs[e], k)
            if masked:
                qpos = q_lo + lax.broadcasted_iota(jnp.int32, (tq, 1), 0)
                kpos = start + lax.broadcasted_iota(jnp.int32, (1, tk), 1)
                s = jnp.where(kpos <= qpos, s, MASK_VALUE)
            s_ref[e] = s
            m_news.append(jnp.broadcast_to(jnp.max(s, -1, keepdims=True), (tq, LANES)))
        for e in range(n_head):
            m = m_ref[e]
            m_new = jnp.maximum(m, m_news[e])
            m_ref[e] = m_new
            a = jnp.exp2(m - m_new)
            for r in range(tq // strip):
                rows = slice(r * strip, (r + 1) * strip)
                part = jnp.zeros((strip, LANES), F32)
                for t in range(tk // LANES):
                    cols = slice(t * LANES, (t + 1) * LANES)
                    p = jnp.exp2(s_ref[e, rows, cols] - m_new[rows])
                    p_ref[e, rows, cols] = p.astype(BF16)
                    part = part + p
                l_ref[e, rows, :] = a[rows] * l_ref[e, rows, :] + part
            acc_ref[e] = a * acc_ref[e] + _dot(p_ref[e], v)

    m_ref[...] = jnp.full(m_ref.shape, MASK_VALUE, F32)
    l_ref[...] = jnp.zeros(l_ref.shape, F32)
    acc_ref[...] = jnp.zeros(acc_ref.shape, F32)
    step(j_diag, True)
    n_live = jnp.int32(0)
    for e in range(n_head):
        qf = qs[e].astype(F32)
        q_norm = jnp.sqrt(jnp.sum(jnp.where(lane < head_dim, qf * qf, 0.0), -1, keepdims=True))
        c_q = jnp.sum(jnp.where((lane >= head_dim) & (lane < head_dim + 3), qf, 0.0), -1, keepdims=True)
        k_norm = jnp.sqrt(kn_ref[0, 0, e:e + 1, 0:1]) * NORM_SLACK
        reach = jnp.max(q_norm * k_norm + c_q - m_ref[e, :, 0:1], 0, keepdims=True)
        live = (blk < j_diag) & (cend_ref[0, 0, e:e + 1, :] * LOG2E <= reach + EXP_UNDERFLOW * LOG2E)
        n_live = jnp.maximum(n_live, jnp.sum(jnp.where(live, 1, 0)))

    def body(t, _):
        step(j_diag - 1 - t, False)
        return 0

    lax.fori_loop(0, n_live, body, 0)
    outs = [acc_ref[e] / jnp.sum(l_ref[e], -1, keepdims=True) for e in range(n_head)]
    o_ref[0] = jnp.where(lane < head_dim, outs[0], outs[1]).astype(BF16)


def _fox_attn(qs, ks, vb, c_keys, kn2, *, q_off, head_dim):
    B, Lq, S = qs.shape
    Lk = ks.shape[1]
    D = vb.shape[2]
    heads = D // head_dim
    tq = _pick_tile(Lq, 512)
    tk = Lk if Lk <= 2048 else _pick_tile(Lk, 512)
    n_blk = Lk // tk
    assert tk % tq == 0 and q_off % tq == 0 and q_off + Lq <= Lk and LANES // head_dim == 2 and n_blk <= LANES
    cend = jnp.swapaxes(c_keys[:, tk - 1::tk, :], 1, 2)
    cend = jnp.pad(cend, ((0, 0), (0, 0), (0, LANES - n_blk))).reshape(B, heads // 2, 2, LANES)
    knb = jnp.broadcast_to(kn2.reshape(B, heads // 2, 2, 1), (B, heads // 2, 2, LANES))
    pair = lambda: pl.BlockSpec((1, 1, 2, LANES), lambda bi, p, i: (bi, p, 0, 0))
    return pl.pallas_call(
        functools.partial(_fox_attn_body, tq=tq, tk=tk, q_off=q_off, head_dim=head_dim, strip=min(tq, 32)),
        grid=(B, D // LANES, Lq // tq),
        in_specs=[pl.BlockSpec((1, tq, 2 * HEAD_SLOT), lambda bi, p, i: (bi, i, p)),
                  pl.BlockSpec((1, Lk, 2 * HEAD_SLOT), lambda bi, p, i: (bi, 0, p)),
                  pl.BlockSpec((1, Lk, LANES), lambda bi, p, i: (bi, 0, p)),
                  pair(), pair()],
        out_specs=pl.BlockSpec((1, tq, LANES), lambda bi, p, i: (bi, i, p)),
        out_shape=jax.ShapeDtypeStruct((B, Lq, D), BF16),
        scratch_shapes=[pltpu.VMEM((2, tq, tk), F32),
                        pltpu.VMEM((2, tq, tk), BF16),
                        pltpu.VMEM((2, tq, LANES), F32),
                        pltpu.VMEM((2, tq, LANES), F32),
                        pltpu.VMEM((2, tq, LANES), F32)],
        compiler_params=_compiler_params(("parallel", "parallel", "arbitrary")),
        name="fox_attn",
    )(qs, ks, vb, cend, knb)


def _fox_weights(w_in, b_f, heads):
    D = w_in.shape[0]
    scale = (D // heads) ** -0.5 * LOG2E
    wf = jnp.pad(w_in[:, 3 * D:], ((0, 0), (0, LANES - heads)))
    w_cat = jnp.concatenate([w_in[:, :D] * scale, w_in[:, D:3 * D], wf], axis=1).astype(BF16)
    bf_pad = jnp.pad(b_f, (0, LANES - heads)).reshape(1, LANES)
    return w_cat, bf_pad


def _sigmoid(x):
    return 1.0 / (1.0 + jnp.exp(-x))


def _hgrn_body(x_ref, win_ref, wo_ref, lbraw_ref, ng_ref, g_ref, b_ref, s0_ref, tri_ref,
               y_ref, sfin_ref, st_ref, qg_ref, kt_ref, kh_ref, qi_ref, vb_ref, eb_ref, o_ref,
               *, alpha, layer, T, heads):
    l = pl.program_id(1)
    D = x_ref.shape[2]
    hd = D // heads
    n_sub = CHUNK // SUBLANES
    n_chunk = T // CHUNK

    @pl.when(l == 0)
    def _():
        for h in range(heads):
            st_ref[h] = s0_ref[0, h].T

    lbr = lbraw_ref[...]
    e = jnp.exp(lbr - jnp.max(lbr, 0, keepdims=True))
    pr = e / jnp.sum(e, 0, keepdims=True)
    lb = jnp.zeros((1, D), F32)
    for r in range(1, layer + 1):
        lb = lb + pr[r:r + 1, :]

    x = x_ref[0]
    hp = _dot(x.astype(BF16), win_ref[...])
    qr, fr, v, gate = hp[:, :D], hp[:, D:2 * D], hp[:, 2 * D:3 * D], hp[:, 3 * D:]
    q = qr * _sigmoid(qr)
    logf = jnp.log(lb + (1.0 - lb) * _sigmoid(fr))
    kk = (1.0 - lb) * _sigmoid(-fr)
    b = _dot_exact_lhs01(tri_ref[...], logf, pieces=2)
    bc = b.reshape(n_chunk, CHUNK, D)
    b_last = jnp.broadcast_to(bc[:, CHUNK - 1:, :], bc.shape).reshape(T, D)

    kh_ref[...] = (kk * jnp.exp(b_last - b)).astype(BF16)
    qi_ref[...] = (q * jnp.exp(b)).astype(BF16)
    vb_ref[...] = v.astype(BF16)
    eb_ref[...] = jnp.exp(b_last)
    mild = jnp.min(b) >= -DECAY_LIMIT

    @pl.when(mild)
    def _():
        kt_ref[...] = (kk * jnp.exp(-b)).astype(BF16)
        row = lax.broadcasted_iota(jnp.int32, (T, T), 0)
        col = lax.broadcasted_iota(jnp.int32, (T, T), 1)
        shift = CHUNK.bit_length() - 1
        keep = (lax.shift_right_logical(row, shift) == lax.shift_right_logical(col, shift)) & (col <= row)
        for h in range(heads):
            sl = slice(h * hd, (h + 1) * hd)
            a = jnp.where(keep, _dot_nt(qi_ref[:, sl], kt_ref[:, sl]), 0.0)
            o_h = _dot(a.astype(BF16), vb_ref[:, sl])
            st = st_ref[h]
            inter = []
            for c in range(n_chunk):
                rc = slice(c * CHUNK, (c + 1) * CHUNK)
                inter.append(_dot_nt(qi_ref[rc, sl], st.astype(BF16)))
                st = st * eb_ref[c * CHUNK:c * CHUNK + 1, sl] + _dot_tn(vb_ref[rc, sl], kh_ref[rc, sl])
            st_ref[h] = st
            o_ref[:, sl] = o_h + jnp.concatenate(inter, 0)

    @pl.when(jnp.logical_not(mild))
    def _():
        shape3 = (T // SUBLANES, SUBLANES, D)
        b3 = b.reshape(shape3)
        lf3 = logf.reshape(shape3)
        b_end = jnp.broadcast_to(b3[:, SUBLANES - 1:, :], shape3).reshape(T, D)
        b_start = jnp.broadcast_to(b3[:, :1, :] - lf3[:, :1, :], shape3).reshape(T, D)
        qt = q * jnp.exp(jnp.minimum(b - b_start, 0.0))
        dm = jnp.exp(jnp.minimum(b_end - b_start, 0.0))
        qg_ref[0] = qt.astype(BF16)
        for gp in range(1, n_sub - 1):
            shifted = jnp.concatenate([jnp.ones((gp * SUBLANES, D), F32), dm[:T - gp * SUBLANES, :]], 0)
            qt = qt * shifted
            qg_ref[gp] = qt.astype(BF16)
        kt_ref[...] = (kk * jnp.exp(b_end - b)).astype(BF16)

        q3 = q.reshape(shape3)
        k3 = kk.reshape(shape3)
        v3 = v.reshape(shape3)
        sub = lax.broadcasted_iota(jnp.int32, (T // SUBLANES, SUBLANES, 1), 1)
        od = [jnp.zeros((T // SUBLANES, SUBLANES, hd), F32) for _ in range(heads)]
        for d in range(SUBLANES):
            kr = k3 if d == 0 else pltpu.roll(k3, d, axis=1)
            br = b3 if d == 0 else pltpu.roll(b3, d, axis=1)
            vr = v3 if d == 0 else pltpu.roll(v3, d, axis=1)
            w = q3 * kr * jnp.exp(jnp.minimum(b3 - br, 0.0))
            for h in range(heads):
                sl = slice(h * hd, (h + 1) * hd)
                a = jnp.sum(w[:, :, sl], -1, keepdims=True)
                od[h] = od[h] + jnp.where(sub >= d, a, 0.0) * vr[:, :, sl]
        for h in range(heads):
            o_ref[:, h * hd:(h + 1) * hd] = od[h].reshape(T, hd)

        rsub = lax.shift_right_logical(lax.broadcasted_iota(jnp.int32, (CHUNK, CHUNK), 0), 3)
        csub = lax.shift_right_logical(lax.broadcasted_iota(jnp.int32, (CHUNK, CHUNK), 1), 3)
        gap = rsub - csub

        def chunk_step(c, _):
            rows = pl.ds(pl.multiple_of(c * CHUNK, CHUNK), CHUNK)
            for h in range(heads):
                sl = slice(h * hd, (h + 1) * hd)
                kt = kt_ref[rows, sl]
                vb = vb_ref[rows, sl]
                a_off = jnp.zeros((CHUNK, CHUNK), F32)
                for gp in range(n_sub - 1):
                    a_g = _dot_nt(qg_ref[gp, rows, sl], kt)
                    a_off = a_off + jnp.where(gap == gp + 1, a_g, 0.0)
                st = st_ref[h]
                o_h = _dot(a_off.astype(BF16), vb) + _dot_nt(qi_ref[rows, sl], st.astype(BF16))
                o_ref[rows, sl] = o_ref[rows, sl] + o_h
                u_t = _dot_tn(vb, kh_ref[rows, sl])
                st_ref[h] = st * eb_ref[pl.ds(c * CHUNK, 1), sl] + u_t
            return 0

        lax.fori_loop(0, n_chunk, chunk_step, 0)

    @pl.when(l == pl.num_programs(1) - 1)
    def _():
        for h in range(heads):
            sfin_ref[0, h] = st_ref[h].T

    outs = []
    for h in range(heads):
        oh = o_ref[:, h * hd:(h + 1) * hd]
        outs.append(oh * lax.rsqrt(jnp.mean(oh * oh, -1, keepdims=True) + RMS_EPS))
    o = jnp.concatenate(outs, -1) * ng_ref[...] * (gate * _sigmoid(gate))
    y = _dot(o.astype(BF16), wo_ref[...])
    y_ref[0] = _layer_norm(alpha * x + y, g_ref[...], b_ref[...])


def _hgrn_layer(x, s0, w_in, w_o, lb_raw, norm_g, g, b, *, alpha, layer):
    B, L, D = x.shape
    heads = D // HGRN_HEAD_DIM
    hd = HGRN_HEAD_DIM
    T = _pick_tile(L, 256)
    assert T % CHUNK == 0
    tri = jnp.asarray(np.kron(np.eye(T // CHUNK, dtype=np.float32), np.tril(np.ones((CHUNK, CHUNK), np.float32))), BF16)
    n_gap = CHUNK // SUBLANES - 1
    return pl.pallas_call(
        functools.partial(_hgrn_body, alpha=alpha, layer=layer, T=T, heads=heads),
        grid=(B, L // T),
        in_specs=[pl.BlockSpec((1, T, D), lambda bi, i: (bi, i, 0)),
                  _resident(w_in.shape), _resident(w_o.shape), _resident(lb_raw.shape),
                  _resident((1, D)), _resident((1, D)), _resident((1, D)),
                  pl.BlockSpec((1, heads, hd, hd), lambda bi, i: (bi, 0, 0, 0)),
                  _resident((T, T))],
        out_specs=[pl.BlockSpec((1, T, D), lambda bi, i: (bi, i, 0)),
                   pl.BlockSpec((1, heads, hd, hd), lambda bi, i: (bi, 0, 0, 0))],
        out_shape=[jax.ShapeDtypeStruct((B, L, D), F32), jax.ShapeDtypeStruct((B, heads, hd, hd), F32)],
        scratch_shapes=[pltpu.VMEM((heads, hd, hd), F32),
                        pltpu.VMEM((n_gap, T, D), BF16),
                        pltpu.VMEM((T, D), BF16),
                        pltpu.VMEM((T, D), BF16),
                        pltpu.VMEM((T, D), BF16),
                        pltpu.VMEM((T, D), BF16),
                        pltpu.VMEM((T, D), F32),
                        pltpu.VMEM((T, D), F32)],
        compiler_params=_compiler_params(("parallel", "arbitrary")),
        name="hgrn_layer",
    )(x, w_in, w_o, lb_raw, norm_g, g, b, s0, tri)


def kernel(x_prompt, x_sample, cache_fox_k, cache_fox_v, cache_fox_logf, state_pool, state_hgrn, pool_w, pool_scale, fox_w_in, fox_b_f, fox_w_o, hgrn_w_in, hgrn_lb_raw, hgrn_norm_g, hgrn_w_o, ln_mix_g, ln_mix_b, w_up, w_down, ln_ffn_g, ln_ffn_b):
    depth = w_up.shape[0]
    alpha = float((2 * depth) ** 0.25)
    B, L, D = x_prompt.shape
    Bs, Ls, _ = x_sample.shape
    past = cache_fox_k.shape[2]
    heads = FOX_HEADS
    hd = D // heads
    pool_state = state_pool.shape[2]
    row = lambda a: a.reshape(1, D)

    xp, xs = x_prompt, x_sample
    pool_p, pool_s = [], []
    fk_p, fv_p, fl_p, fk_s, fv_s, fl_s = [], [], [], [], [], []
    hg_p, hg_s = [], []
    for i in range(depth):
        kind, j = i % N_MIXERS, i // N_MIXERS
        g_mix, b_mix = row(ln_mix_g[i]), row(ln_mix_b[i])
        mlp = (w_up[i].astype(BF16), w_down[i].astype(BF16), row(ln_ffn_g[i]), row(ln_ffn_b[i]))
        if kind == 0:
            w = pool_w[j].astype(BF16)
            scale = row(pool_scale[j])
            hist_p = jnp.zeros((B, POOL_HIST, D), F32)
            hist_s = jnp.pad(state_pool[j], ((0, 0), (POOL_HIST - pool_state, 0), (0, 0)))
            pool_p.append(xp[:, L - pool_state:])
            pool_s.append(xs[:, Ls - pool_state:])
            xp = _pool_mlp(xp, hist_p, w, scale, g_mix, b_mix, mlp, alpha=alpha, pos0=0)
            xs = _pool_mlp(xs, hist_s, w, scale, g_mix, b_mix, mlp, alpha=alpha, pos0=past)
        elif kind == 1:
            w_cat, bf_pad = _fox_weights(fox_w_in[j], fox_b_f[j], heads)
            w_o = fox_w_o[j].astype(BF16)
            k, v, lf, c, kn, qs, ks, vb = _fox_proj(xp, w_cat, bf_pad, jnp.zeros((B, 1, LANES), F32), heads=heads)
            o = _fox_attn(qs, ks, vb, c, jnp.max(kn, (1, 2))[:, :heads], q_off=0, head_dim=hd)
            xp = _proj_mlp(o.reshape(B * L, D), w_o, xp.reshape(B * L, D), g_mix, b_mix, mlp,
                           alpha=alpha).reshape(B, L, D)
            fk_p.append(k.reshape(B, L, heads, hd))
            fv_p.append(v.reshape(B, L, heads, hd))
            fl_p.append(lf)
            lf_pad = jnp.pad(cache_fox_logf[j], ((0, 0), (0, 0), (0, LANES - heads)))
            ks_c, vb_c, c_c, kn_c, c_end = _fox_cache(cache_fox_k[j], cache_fox_v[j], lf_pad, heads=heads)
            k, v, lf, c, kn, qs, ks, vb = _fox_proj(xs, w_cat, bf_pad, c_end, heads=heads)
            tail = (-(past + Ls)) % LANES
            grow = ((0, 0), (0, tail), (0, 0))
            ks_all = jnp.pad(jnp.concatenate([ks_c, ks], 1), grow)
            vb_all = jnp.pad(jnp.concatenate([vb_c, vb], 1), grow)
            c_all = jnp.pad(jnp.concatenate([c_c, c], 1), grow, mode="edge")
            kn_all = jnp.maximum(jnp.max(kn_c, (1, 2)), jnp.max(kn, (1, 2)))[:, :heads]
            o = _fox_attn(qs, ks_all, vb_all, c_all, kn_all, q_off=past, head_dim=hd)
            xs = _proj_mlp(o.reshape(Bs * Ls, D), w_o, xs.reshape(Bs * Ls, D), g_mix, b_mix, mlp,
                           alpha=alpha).reshape(Bs, Ls, D)
            fk_s.append(k.reshape(Bs, Ls, heads, hd))
            fv_s.append(v.reshape(Bs, Ls, heads, hd))
            fl_s.append(lf)
        else:
            w_in = hgrn_w_in[j].astype(BF16)
            w_o = hgrn_w_o[j].astype(BF16)
            ng = row(hgrn_norm_g[j])
            s0 = jnp.zeros((B,) + state_hgrn.shape[2:], F32)
            xp, sp = _hgrn_layer(xp, s0, w_in, w_o, hgrn_lb_raw, ng, g_mix, b_mix, alpha=alpha, layer=i)
            xs, ss = _hgrn_layer(xs, state_hgrn[j], w_in, w_o, hgrn_lb_raw, ng, g_mix, b_mix, alpha=alpha, layer=i)
            hg_p.append(sp)
            hg_s.append(ss)
            xp = _mlp_ln(xp.reshape(B * L, D), mlp, alpha=alpha).reshape(B, L, D)
            xs = _mlp_ln(xs.reshape(Bs * Ls, D), mlp, alpha=alpha).reshape(Bs, Ls, D)
    return (xp, xs, jnp.stack(pool_p), jnp.stack(pool_s),
            jnp.stack(fk_p), jnp.stack(fv_p), jnp.stack(fl_p),
            jnp.stack(fk_s), jnp.stack(fv_s), jnp.stack(fl_s),
            jnp.stack(hg_p), jnp.stack(hg_s))
```

```python
import functools

import jax
import jax.numpy as jnp
import numpy as np
from jax import lax
from jax.experimental import pallas as pl
from jax.experimental.pallas import tpu as pltpu

F32 = jnp.float32
BF16 = jnp.bfloat16

CHUNK = 64
N_MIXERS = 3
POOL_WINDOWS = (2, 4, 8, 16)
POOL_HIST = 16
FOX_HEADS = 16
HGRN_HEAD_DIM = 128
LN_EPS = 1e-5
RMS_EPS = 1e-6

LANES = 128
SUBLANES = 8
VMEM_LIMIT_BYTES = 56 * 1024 * 1024

HEAD_SLOT = LANES
MASK_VALUE = -1e30
LOG2E = float(np.log2(np.e))
EXP_UNDERFLOW = 90.0
NORM_SLACK = 1.02
DECAY_LIMIT = 60.0


def _compiler_params(semantics):
    return pltpu.CompilerParams(dimension_semantics=semantics, vmem_limit_bytes=VMEM_LIMIT_BYTES)


def _resident(shape):
    return pl.BlockSpec(shape, lambda *_: (0,) * len(shape), pipeline_mode=pl.Buffered(1))


def _pick_tile(n, pref):
    t = min(n, pref)
    while n % t:
        t //= 2
    return t


def _layer_norm(z, g, b):
    mu = jnp.mean(z, -1, keepdims=True)
    zc = z - mu
    var = jnp.mean(zc * zc, -1, keepdims=True)
    return zc * lax.rsqrt(var + LN_EPS) * g + b


def _split3(x):
    hi = x.astype(BF16)
    r1 = x - hi.astype(F32)
    mid = r1.astype(BF16)
    lo = (r1 - mid.astype(F32)).astype(BF16)
    return hi, mid, lo


def _dot(a, b):
    return jnp.dot(a, b, preferred_element_type=F32)


def _dot_nt(a, b):
    return lax.dot_general(a, b, (((1,), (1,)), ((), ())), preferred_element_type=F32)


def _dot_tn(a, b):
    return lax.dot_general(a, b, (((0,), (0,)), ((), ())), preferred_element_type=F32)


def _dot_exact_lhs01(m01, x, pieces=3):
    hi, mid, lo = _split3(x)
    y = _dot(m01, hi) + _dot(m01, mid)
    return y + _dot(m01, lo) if pieces == 3 else y


FF_CHUNK = 512


def _mlp_ln_value(x, wup_ref, wdn_ref, g, b, alpha):
    xb = x.astype(BF16)
    d_ff = wup_ref.shape[1]
    ff_chunk = _pick_tile(d_ff, FF_CHUNK)
    acc = jnp.zeros(x.shape, F32)
    for c in range(d_ff // ff_chunk):
        sl = slice(c * ff_chunk, (c + 1) * ff_chunk)
        h = jnp.maximum(_dot(xb, wup_ref[:, sl]), 0.0)
        acc = acc + _dot((h * h).astype(BF16), wdn_ref[sl, :])
    return _layer_norm(alpha * x + acc, g, b)


def _mlp_specs(D, d_ff):
    return [_resident((D, d_ff)), _resident((d_ff, D)), _resident((1, D)), _resident((1, D))]


def _mlp_ln_body(x_ref, wup_ref, wdn_ref, g_ref, b_ref, o_ref, *, alpha):
    o_ref[...] = _mlp_ln_value(x_ref[...], wup_ref, wdn_ref, g_ref[...], b_ref[...], alpha)


def _mlp_ln(x2, mlp, *, alpha):
    R, D = x2.shape
    tm = _pick_tile(R, 512)
    return pl.pallas_call(
        functools.partial(_mlp_ln_body, alpha=alpha),
        grid=(R // tm,),
        in_specs=[pl.BlockSpec((tm, D), lambda i: (i, 0))] + _mlp_specs(D, mlp[0].shape[1]),
        out_specs=pl.BlockSpec((tm, D), lambda i: (i, 0)),
        out_shape=jax.ShapeDtypeStruct((R, D), F32),
        compiler_params=_compiler_params(("parallel",)),
        name="mlp_ln",
    )(x2, *mlp)


def _proj_mlp_body(a_ref, w_ref, x_ref, g_ref, b_ref, wup_ref, wdn_ref, g2_ref, b2_ref, o_ref, *, alpha):
    y = _dot(a_ref[...], w_ref[...])
    x1 = _layer_norm(alpha * x_ref[...] + y, g_ref[...], b_ref[...])
    o_ref[...] = _mlp_ln_value(x1, wup_ref, wdn_ref, g2_ref[...], b2_ref[...], alpha)


def _proj_mlp(a2, w, x2, g, b, mlp, *, alpha):
    R, D = x2.shape
    tm = _pick_tile(R, 512)
    return pl.pallas_call(
        functools.partial(_proj_mlp_body, alpha=alpha),
        grid=(R // tm,),
        in_specs=[pl.BlockSpec((tm, D), lambda i: (i, 0)),
                  _resident((D, D)),
                  pl.BlockSpec((tm, D), lambda i: (i, 0)),
                  _resident((1, D)), _resident((1, D))] + _mlp_specs(D, mlp[0].shape[1]),
        out_specs=pl.BlockSpec((tm, D), lambda i: (i, 0)),
        out_shape=jax.ShapeDtypeStruct((R, D), F32),
        compiler_params=_compiler_params(("parallel",)),
        name="proj_mlp",
    )(a2, w, x2, g, b, *mlp)


def _pool_mlp_body(x_ref, xprev_ref, hist_ref, w_ref, scale_ref, g_ref, b_ref, wup_ref, wdn_ref, g2_ref, b2_ref,
                   o_ref, xa_ref, x1_ref, *, alpha, pos0, tm, n_seq):
    i = pl.program_id(1)
    group = x_ref.shape[2] // len(POOL_WINDOWS)
    n_avail = lax.broadcasted_iota(jnp.int32, (tm, 1), 0) + (i * tm + pos0 + 1)
    for q in range(n_seq):
        x = x_ref[q]
        xa_ref[0:POOL_HIST, :] = jnp.where(i == 0, hist_ref[q], xprev_ref[q])
        xa_ref[POOL_HIST:, :] = x
        ys = []
        for gi, w in enumerate(POOL_WINDOWS):
            sl = slice(gi * group, (gi + 1) * group)
            s = xa_ref[:, sl]
            sh = 1
            while sh < w:
                s = s + pltpu.roll(s, sh, axis=0)
                sh *= 2
            cnt = jnp.minimum(n_avail, w).astype(F32)
            pooled = s[POOL_HIST:, :] / cnt - x[:, sl]
            ys.append(_dot(pooled.astype(BF16), w_ref[gi]))
        y = jnp.concatenate(ys, -1) * scale_ref[...]
        x1_ref[q * tm:(q + 1) * tm, :] = _layer_norm(alpha * x + y, g_ref[...], b_ref[...])
    out = _mlp_ln_value(x1_ref[...], wup_ref, wdn_ref, g2_ref[...], b2_ref[...], alpha)
    for q in range(n_seq):
        o_ref[q] = out[q * tm:(q + 1) * tm, :]


def _pool_mlp(x, hist, w, scale, g, b, mlp, *, alpha, pos0):
    B, L, D = x.shape
    n_g, group = w.shape[0], w.shape[1]
    tm = _pick_tile(L, 512)
    n_seq = _pick_tile(B, 512 // tm) if tm == L else 1
    assert tm % POOL_HIST == 0 and max(POOL_WINDOWS) <= POOL_HIST
    per = tm // POOL_HIST
    return pl.pallas_call(
        functools.partial(_pool_mlp_body, alpha=alpha, pos0=pos0, tm=tm, n_seq=n_seq),
        grid=(B // n_seq, L // tm),
        in_specs=[pl.BlockSpec((n_seq, tm, D), lambda bi, i: (bi, i, 0)),
                  pl.BlockSpec((n_seq, POOL_HIST, D), lambda bi, i: (bi, jnp.maximum(i * per - 1, 0), 0)),
                  pl.BlockSpec((n_seq, POOL_HIST, D), lambda bi, i: (bi, 0, 0)),
                  _resident((n_g, group, group)),
                  _resident((1, D)), _resident((1, D)), _resident((1, D))] + _mlp_specs(D, mlp[0].shape[1]),
        out_specs=pl.BlockSpec((n_seq, tm, D), lambda bi, i: (bi, i, 0)),
        out_shape=jax.ShapeDtypeStruct((B, L, D), F32),
        scratch_shapes=[pltpu.VMEM((tm + POOL_HIST, D), F32), pltpu.VMEM((n_seq * tm, D), F32)],
        compiler_params=_compiler_params(("parallel", "arbitrary")),
        name="pool_mlp",
    )(x, x, hist, w, scale, g, b, *mlp)


def _placement_matrices(heads, head_dim):
    assert 3 * heads < LANES and head_dim + 6 <= HEAD_SLOT
    pq = np.zeros((LANES, heads * HEAD_SLOT), np.float32)
    pk = np.zeros((LANES, heads * HEAD_SLOT), np.float32)
    one = 3 * heads
    for h in range(heads):
        base = h * HEAD_SLOT + head_dim
        for p in range(3):
            pq[p * heads + h, base + p] = 1.0
            pq[one, base + 3 + p] = 1.0
            pk[one, base + p] = 1.0
            pk[p * heads + h, base + 3 + p] = -1.0
    return jnp.asarray(pq, BF16), jnp.asarray(pk, BF16)


def _bias_parts(c, heads):
    hi, mid, lo = _split3(c)
    lane = lax.broadcasted_iota(jnp.int32, c.shape, 1)
    parts = (hi.astype(F32) + pltpu.roll(mid.astype(F32), heads, axis=1)
             + pltpu.roll(lo.astype(F32), 2 * heads, axis=1) + jnp.where(lane == 3 * heads, 1.0, 0.0))
    return parts.astype(BF16)


def _slotted(nat, aug, head_dim):
    T = nat.shape[0]
    per = LANES // head_dim
    assert per == 2
    lane = lax.broadcasted_iota(jnp.int32, (T, LANES), 1)
    low = lane < head_dim
    out = []
    for p in range(nat.shape[1] // LANES):
        pair = nat[:, p * LANES:(p + 1) * LANES]
        swapped = pltpu.roll(pair, head_dim, axis=1)
        out.append(jnp.where(low, pair, aug[:, (2 * p) * HEAD_SLOT:(2 * p + 1) * HEAD_SLOT]))
        out.append(jnp.where(low, swapped, aug[:, (2 * p + 1) * HEAD_SLOT:(2 * p + 2) * HEAD_SLOT]))
    return jnp.concatenate(out, -1).astype(BF16)


def _log_sigmoid(x):
    return jnp.minimum(x, 0.0) - jnp.log(1.0 + jnp.exp(-jnp.abs(x)))


def _key_norm_matrix(heads, head_dim):
    g = np.zeros((heads * head_dim, LANES), np.float32)
    for h in range(heads):
        g[h * head_dim:(h + 1) * head_dim, h] = 1.0
    return jnp.asarray(g, BF16)


def _max_sq_key_norm(k, gn):
    return jnp.max(_dot((k * k).astype(BF16), gn), 0, keepdims=True)


def _fox_proj_body(x_ref, w_ref, bf_ref, c0_ref, tri_ref, pq_ref, pk_ref, gn_ref,
                   k_ref, v_ref, lf_ref, c_ref, kn_ref, qs_ref, ks_ref, vb_ref, carry_ref, *, heads, transposed):
    l = pl.program_id(1)
    D = x_ref.shape[2]

    @pl.when(l == 0)
    def _():
        carry_ref[...] = c0_ref[0]

    h = _dot(x_ref[0].astype(BF16), w_ref[...])
    q, k, v = h[:, :D], h[:, D:2 * D], h[:, 2 * D:3 * D]
    fr = h[:, 3 * D:] + bf_ref[...]
    lane = lax.broadcasted_iota(jnp.int32, fr.shape, 1)
    logf = jnp.where(lane < heads, _log_sigmoid(fr), 0.0)
    c = _dot_exact_lhs01(tri_ref[...], logf) + carry_ref[...]
    carry_ref[...] = c[c.shape[0] - 1:, :]
    parts = _bias_parts(c * LOG2E, heads)
    hd = D // heads
    k_ref[0] = k.T if transposed else k
    v_ref[0] = v.T if transposed else v
    vb_ref[0] = v.astype(BF16)
    lf_ref[0] = logf[:, :heads]
    c_ref[0] = c[:, :heads]
    kn_ref[0, 0] = _max_sq_key_norm(k, gn_ref[...])
    qs_ref[0] = _slotted(q, _dot(parts, pq_ref[...]), hd)
    ks_ref[0] = _slotted(k, _dot(parts, pk_ref[...]), hd)


def _fox_proj(x, w_cat, bf_pad, c0, *, heads):
    B, L, D = x.shape
    T = _pick_tile(L, 256)
    transposed = T % LANES == 0
    kv_spec = pl.BlockSpec((1, D, T), lambda bi, i: (bi, 0, i)) if transposed else None
    kv_shape = jax.ShapeDtypeStruct((B, D, L) if transposed else (B, L, D), F32)
    tri = jnp.asarray(np.tril(np.ones((T, T), np.float32)), BF16)
    pq, pk = _placement_matrices(heads, D // heads)
    gn = _key_norm_matrix(heads, D // heads)
    S = heads * HEAD_SLOT
    row = lambda n: pl.BlockSpec((1, T, n), lambda bi, i: (bi, i, 0))
    k, v, *rest = pl.pallas_call(
        functools.partial(_fox_proj_body, heads=heads, transposed=transposed),
        grid=(B, L // T),
        in_specs=[row(D), _resident(w_cat.shape), _resident((1, LANES)),
                  pl.BlockSpec((1, 1, LANES), lambda bi, i: (bi, 0, 0)),
                  _resident((T, T)), _resident((LANES, S)), _resident((LANES, S)), _resident((D, LANES))],
        out_specs=[kv_spec or row(D), kv_spec or row(D), row(heads), row(heads),
                   pl.BlockSpec((1, 1, 1, LANES), lambda bi, i: (bi, i, 0, 0)),
                   row(S), row(S), row(D)],
        out_shape=[kv_shape, kv_shape,
                   jax.ShapeDtypeStruct((B, L, heads), F32), jax.ShapeDtypeStruct((B, L, heads), F32),
                   jax.ShapeDtypeStruct((B, L // T, 1, LANES), F32),
                   jax.ShapeDtypeStruct((B, L, S), BF16), jax.ShapeDtypeStruct((B, L, S), BF16),
                   jax.ShapeDtypeStruct((B, L, D), BF16)],
        scratch_shapes=[pltpu.VMEM((1, LANES), F32)],
        compiler_params=_compiler_params(("parallel", "arbitrary")),
        name="fox_proj",
    )(x, w_cat, bf_pad, c0, tri, pq, pk, gn)
    hd = D // heads
    if transposed:
        k, v = (jnp.transpose(a.reshape(B, heads, hd, L), (0, 3, 1, 2)) for a in (k, v))
    else:
        k, v = k.reshape(B, L, heads, hd), v.reshape(B, L, heads, hd)
    return (k, v, *rest)


def _fox_cache_body(k_ref, v_ref, lf_ref, tri_ref, pk_ref, gn_ref,
                    ks_ref, vb_ref, c_ref, kn_ref, cend_ref, carry_ref, *, heads):
    l = pl.program_id(1)

    @pl.when(l == 0)
    def _():
        carry_ref[...] = jnp.zeros(carry_ref.shape, F32)

    D = k_ref.shape[2]
    k = k_ref[0]
    c = _dot_exact_lhs01(tri_ref[...], lf_ref[0]) + carry_ref[...]
    last = c[c.shape[0] - 1:, :]
    carry_ref[...] = last
    cend_ref[0] = last
    c_ref[0] = c[:, :heads]
    kn_ref[0, 0] = _max_sq_key_norm(k, gn_ref[...])
    parts = _bias_parts(c * LOG2E, heads)
    ks_ref[0] = _slotted(k, _dot(parts, pk_ref[...]), D // heads)
    vb_ref[0] = v_ref[0].astype(BF16)


def _fox_cache(k, v, lf_pad, *, heads):
    B, P, D = k.shape
    T = _pick_tile(P, 256)
    tri = jnp.asarray(np.tril(np.ones((T, T), np.float32)), BF16)
    _, pk = _placement_matrices(heads, D // heads)
    gn = _key_norm_matrix(heads, D // heads)
    S = heads * HEAD_SLOT
    row = lambda n: pl.BlockSpec((1, T, n), lambda bi, i: (bi, i, 0))
    return pl.pallas_call(
        functools.partial(_fox_cache_body, heads=heads),
        grid=(B, P // T),
        in_specs=[row(D), row(D), row(LANES), _resident((T, T)), _resident((LANES, S)), _resident((D, LANES))],
        out_specs=[row(S), row(D), row(heads),
                   pl.BlockSpec((1, 1, 1, LANES), lambda bi, i: (bi, i, 0, 0)),
                   pl.BlockSpec((1, 1, LANES), lambda bi, i: (bi, 0, 0))],
        out_shape=[jax.ShapeDtypeStruct((B, P, S), BF16), jax.ShapeDtypeStruct((B, P, D), BF16),
                   jax.ShapeDtypeStruct((B, P, heads), F32),
                   jax.ShapeDtypeStruct((B, P // T, 1, LANES), F32),
                   jax.ShapeDtypeStruct((B, 1, LANES), F32)],
        scratch_shapes=[pltpu.VMEM((1, LANES), F32)],
        compiler_params=_compiler_params(("parallel", "arbitrary")),
        name="fox_cache",
    )(k, v, lf_pad, tri, pk, gn)


def _fox_attn_body(q_ref, k_ref, v_ref, cend_ref, kn_ref, o_ref, s_ref, p_ref, m_ref, l_ref, acc_ref,
                   *, tq, tk, q_off, head_dim, strip):
    i = pl.program_id(2)
    q_lo = q_off + i * tq
    j_diag = (q_lo + tq - 1) // tk
    lane = lax.broadcasted_iota(jnp.int32, (tq, LANES), 1)
    blk = lax.broadcasted_iota(jnp.int32, (1, LANES), 1)
    n_head = LANES // head_dim
    qs = [q_ref[0, :, e * HEAD_SLOT:(e + 1) * HEAD_SLOT] for e in range(n_head)]

    def step(j, masked):
        start = pl.multiple_of(j * tk, tk)
        v = v_ref[0, pl.ds(start, tk), :]
        m_news = []
        for e in range(n_head):
            k = k_ref[0, pl.ds(start, tk), e * HEAD_SLOT:(e + 1) * HEAD_SLOT]
            s = _dot_nt(qs[e], k)
            if masked:
                qpos = q_lo + lax.broadcasted_iota(jnp.int32, (tq, 1), 0)
                kpos = start + lax.broadcasted_iota(jnp.int32, (1, tk), 1)
                s = jnp.where(kpos <= qpos, s, MASK_VALUE)
            s_ref[e] = s
            m_news.append(jnp.broadcast_to(jnp.max(s, -1, keepdims=True), (tq, LANES)))
        for e in range(n_head):
            m = m_ref[e]
            m_new = jnp.maximum(m, m_news[e])
            m_ref[e] = m_new
            a = jnp.exp2(m - m_new)
            for r in range(tq // strip):
                rows = slice(r * strip, (r + 1) * strip)
                part = jnp.zeros((strip, LANES), F32)
                for t in range(tk // LANES):
                    cols = slice(t * LANES, (t + 1) * LANES)
                    p = jnp.exp2(s_ref[e, rows, cols] - m_new[rows])
                    p_ref[e, rows, cols] = p.astype(BF16)
                    part = part + p
                l_ref[e, rows, :] = a[rows] * l_ref[e, rows, :] + part
            acc_ref[e] = a * acc_ref[e] + _dot(p_ref[e], v)

    m_ref[...] = jnp.full(m_ref.shape, MASK_VALUE, F32)
    l_ref[...] = jnp.zeros(l_ref.shape, F32)
    acc_ref[...] = jnp.zeros(acc_ref.shape, F32)
    step(j_diag, True)
    n_live = jnp.int32(0)
    for e in range(n_head):
        qf = qs[e].astype(F32)
        q_norm = jnp.sqrt(jnp.sum(jnp.where(lane < head_dim, qf * qf, 0.0), -1, keepdims=True))
        c_q = jnp.sum(jnp.where((lane >= head_dim) & (lane < head_dim + 3), qf, 0.0), -1, keepdims=True)
        k_norm = jnp.sqrt(kn_ref[0, 0, e:e + 1, 0:1]) * NORM_SLACK
        reach = jnp.max(q_norm * k_norm + c_q - m_ref[e, :, 0:1], 0, keepdims=True)
        live = (blk < j_diag) & (cend_ref[0, 0, e:e + 1, :] * LOG2E <= reach + EXP_UNDERFLOW * LOG2E)
        n_live = jnp.maximum(n_live, jnp.sum(jnp.where(live, 1, 0)))

    def body(t, _):
        step(j_diag - 1 - t, False)
        return 0

    lax.fori_loop(0, n_live, body, 0)
    outs = [acc_ref[e] / jnp.sum(l_ref[e], -1, keepdims=True) for e in range(n_head)]
    o_ref[0] = jnp.where(lane < head_dim, outs[0], outs[1]).astype(BF16)


def _fox_attn(qs, ks, vb, c_keys, kn2, *, q_off, head_dim):
    B, Lq, S = qs.shape
    Lk = ks.shape[1]
    D = vb.shape[2]
    heads = D // head_dim
    tq = _pick_tile(Lq, 512)
    tk = Lk if Lk <= 2048 else _pick_tile(Lk, 512)
    n_blk = Lk // tk
    assert tk % tq == 0 and q_off % tq == 0 and q_off + Lq <= Lk and LANES // head_dim == 2 and n_blk <= LANES
    cend = jnp.swapaxes(c_keys[:, tk - 1::tk, :], 1, 2)
    cend = jnp.pad(cend, ((0, 0), (0, 0), (0, LANES - n_blk))).reshape(B, heads // 2, 2, LANES)
    knb = jnp.broadcast_to(kn2.reshape(B, heads // 2, 2, 1), (B, heads // 2, 2, LANES))
    pair = lambda: pl.BlockSpec((1, 1, 2, LANES), lambda bi, p, i: (bi, p, 0, 0))
    return pl.pallas_call(
        functools.partial(_fox_attn_body, tq=tq, tk=tk, q_off=q_off, head_dim=head_dim, strip=min(tq, 32)),
        grid=(B, D // LANES, Lq // tq),
        in_specs=[pl.BlockSpec((1, tq, 2 * HEAD_SLOT), lambda bi, p, i: (bi, i, p)),
                  pl.BlockSpec((1, Lk, 2 * HEAD_SLOT), lambda bi, p, i: (bi, 0, p)),
                  pl.BlockSpec((1, Lk, LANES), lambda bi, p, i: (bi, 0, p)),
                  pair(), pair()],
        out_specs=pl.BlockSpec((1, tq, LANES), lambda bi, p, i: (bi, i, p)),
        out_shape=jax.ShapeDtypeStruct((B, Lq, D), BF16),
        scratch_shapes=[pltpu.VMEM((2, tq, tk), F32),
                        pltpu.VMEM((2, tq, tk), BF16),
                        pltpu.VMEM((2, tq, LANES), F32),
                        pltpu.VMEM((2, tq, LANES), F32),
                        pltpu.VMEM((2, tq, LANES), F32)],
        compiler_params=_compiler_params(("parallel", "parallel", "arbitrary")),
        name="fox_attn",
    )(qs, ks, vb, cend, knb)


def _fox_weights(w_in, b_f, heads):
    D = w_in.shape[0]
    scale = (D // heads) ** -0.5 * LOG2E
    wf = jnp.pad(w_in[:, 3 * D:], ((0, 0), (0, LANES - heads)))
    w_cat = jnp.concatenate([w_in[:, :D] * scale, w_in[:, D:3 * D], wf], axis=1).astype(BF16)
    bf_pad = jnp.pad(b_f, (0, LANES - heads)).reshape(1, LANES)
    return w_cat, bf_pad


def _sigmoid(x):
    return 1.0 / (1.0 + jnp.exp(-x))


def _hgrn_body(x_ref, win_ref, wo_ref, lbraw_ref, ng_ref, g_ref, b_ref, s0_ref, tri_ref,
               y_ref, sfin_ref, st_ref, qg_ref, kt_ref, kh_ref, qi_ref, vb_ref, eb_ref, o_ref,
               *, alpha, layer, T, heads):
    l = pl.program_id(1)
    D = x_ref.shape[2]
    hd = D // heads
    n_sub = CHUNK // SUBLANES
    n_chunk = T // CHUNK

    @pl.when(l == 0)
    def _():
        for h in range(heads):
            st_ref[h] = s0_ref[0, h].T

    lbr = lbraw_ref[...]
    e = jnp.exp(lbr - jnp.max(lbr, 0, keepdims=True))
    pr = e / jnp.sum(e, 0, keepdims=True)
    lb = jnp.zeros((1, D), F32)
    for r in range(1, layer + 1):
        lb = lb + pr[r:r + 1, :]

    x = x_ref[0]
    hp = _dot(x.astype(BF16), win_ref[...])
    qr, fr, v, gate = hp[:, :D], hp[:, D:2 * D], hp[:, 2 * D:3 * D], hp[:, 3 * D:]
    q = qr * _sigmoid(qr)
    logf = jnp.log(lb + (1.0 - lb) * _sigmoid(fr))
    kk = (1.0 - lb) * _sigmoid(-fr)
    b = _dot_exact_lhs01(tri_ref[...], logf, pieces=2)
    bc = b.reshape(n_chunk, CHUNK, D)
    b_last = jnp.broadcast_to(bc[:, CHUNK - 1:, :], bc.shape).reshape(T, D)

    kh_ref[...] = (kk * jnp.exp(b_last - b)).astype(BF16)
    qi_ref[...] = (q * jnp.exp(b)).astype(BF16)
    vb_ref[...] = v.astype(BF16)
    eb_ref[...] = jnp.exp(b_last)
    mild = jnp.min(b) >= -DECAY_LIMIT

    @pl.when(mild)
    def _():
        kt_ref[...] = (kk * jnp.exp(-b)).astype(BF16)
        row = lax.broadcasted_iota(jnp.int32, (T, T), 0)
        col = lax.broadcasted_iota(jnp.int32, (T, T), 1)
        shift = CHUNK.bit_length() - 1
        keep = (lax.shift_right_logical(row, shift) == lax.shift_right_logical(col, shift)) & (col <= row)
        for h in range(heads):
            sl = slice(h * hd, (h + 1) * hd)
            a = jnp.where(keep, _dot_nt(qi_ref[:, sl], kt_ref[:, sl]), 0.0)
            o_h = _dot(a.astype(BF16), vb_ref[:, sl])
            st = st_ref[h]
            inter = []
            for c in range(n_chunk):
                rc = slice(c * CHUNK, (c + 1) * CHUNK)
                inter.append(_dot_nt(qi_ref[rc, sl], st.astype(BF16)))
                st = st * eb_ref[c * CHUNK:c * CHUNK + 1, sl] + _dot_tn(vb_ref[rc, sl], kh_ref[rc, sl])
            st_ref[h] = st
            o_ref[:, sl] = o_h + jnp.concatenate(inter, 0)

    @pl.when(jnp.logical_not(mild))
    def _():
        shape3 = (T // SUBLANES, SUBLANES, D)
        b3 = b.reshape(shape3)
        lf3 = logf.reshape(shape3)
        b_end = jnp.broadcast_to(b3[:, SUBLANES - 1:, :], shape3).reshape(T, D)
        b_start = jnp.broadcast_to(b3[:, :1, :] - lf3[:, :1, :], shape3).reshape(T, D)
        qt = q * jnp.exp(jnp.minimum(b - b_start, 0.0))
        dm = jnp.exp(jnp.minimum(b_end - b_start, 0.0))
        qg_ref[0] = qt.astype(BF16)
        for gp in range(1, n_sub - 1):
            shifted = jnp.concatenate([jnp.ones((gp * SUBLANES, D), F32), dm[:T - gp * SUBLANES, :]], 0)
            qt = qt * shifted
            qg_ref[gp] = qt.astype(BF16)
        kt_ref[...] = (kk * jnp.exp(b_end - b)).astype(BF16)

        q3 = q.reshape(shape3)
        k3 = kk.reshape(shape3)
        v3 = v.reshape(shape3)
        sub = lax.broadcasted_iota(jnp.int32, (T // SUBLANES, SUBLANES, 1), 1)
        od = [jnp.zeros((T // SUBLANES, SUBLANES, hd), F32) for _ in range(heads)]
        for d in range(SUBLANES):
            kr = k3 if d == 0 else pltpu.roll(k3, d, axis=1)
            br = b3 if d == 0 else pltpu.roll(b3, d, axis=1)
            vr = v3 if d == 0 else pltpu.roll(v3, d, axis=1)
            w = q3 * kr * jnp.exp(jnp.minimum(b3 - br, 0.0))
            for h in range(heads):
                sl = slice(h * hd, (h + 1) * hd)
                a = jnp.sum(w[:, :, sl], -1, keepdims=True)
                od[h] = od[h] + jnp.where(sub >= d, a, 0.0) * vr[:, :, sl]
        for h in range(heads):
            o_ref[:, h * hd:(h + 1) * hd] = od[h].reshape(T, hd)

        rsub = lax.shift_right_logical(lax.broadcasted_iota(jnp.int32, (CHUNK, CHUNK), 0), 3)
        csub = lax.shift_right_logical(lax.broadcasted_iota(jnp.int32, (CHUNK, CHUNK), 1), 3)
        gap = rsub - csub

        def chunk_step(c, _):
            rows = pl.ds(pl.multiple_of(c * CHUNK, CHUNK), CHUNK)
            for h in range(heads):
                sl = slice(h * hd, (h + 1) * hd)
                kt = kt_ref[rows, sl]
                vb = vb_ref[rows, sl]
                a_off = jnp.zeros((CHUNK, CHUNK), F32)
                for gp in range(n_sub - 1):
                    a_g = _dot_nt(qg_ref[gp, rows, sl], kt)
                    a_off = a_off + jnp.where(gap == gp + 1, a_g, 0.0)
                st = st_ref[h]
                o_h = _dot(a_off.astype(BF16), vb) + _dot_nt(qi_ref[rows, sl], st.astype(BF16))
                o_ref[rows, sl] = o_ref[rows, sl] + o_h
                u_t = _dot_tn(vb, kh_ref[rows, sl])
                st_ref[h] = st * eb_ref[pl.ds(c * CHUNK, 1), sl] + u_t
            return 0

        lax.fori_loop(0, n_chunk, chunk_step, 0)

    @pl.when(l == pl.num_programs(1) - 1)
    def _():
        for h in range(heads):
            sfin_ref[0, h] = st_ref[h].T

    outs = []
    for h in range(heads):
        oh = o_ref[:, h * hd:(h + 1) * hd]
        outs.append(oh * lax.rsqrt(jnp.mean(oh * oh, -1, keepdims=True) + RMS_EPS))
    o = jnp.concatenate(outs, -1) * ng_ref[...] * (gate * _sigmoid(gate))
    y = _dot(o.astype(BF16), wo_ref[...])
    y_ref[0] = _layer_norm(alpha * x + y, g_ref[...], b_ref[...])


def _hgrn_layer(x, s0, w_in, w_o, lb_raw, norm_g, g, b, *, alpha, layer):
    B, L, D = x.shape
    heads = D // HGRN_HEAD_DIM
    hd = HGRN_HEAD_DIM
    T = _pick_tile(L, 256)
    assert T % CHUNK == 0
    tri = jnp.asarray(np.kron(np.eye(T // CHUNK, dtype=np.float32), np.tril(np.ones((CHUNK, CHUNK), np.float32))), BF16)
    n_gap = CHUNK // SUBLANES - 1
    return pl.pallas_call(
        functools.partial(_hgrn_body, alpha=alpha, layer=layer, T=T, heads=heads),
        grid=(B, L // T),
        in_specs=[pl.BlockSpec((1, T, D), lambda bi, i: (bi, i, 0)),
                  _resident(w_in.shape), _resident(w_o.shape), _resident(lb_raw.shape),
                  _resident((1, D)), _resident((1, D)), _resident((1, D)),
                  pl.BlockSpec((1, heads, hd, hd), lambda bi, i: (bi, 0, 0, 0)),
                  _resident((T, T))],
        out_specs=[pl.BlockSpec((1, T, D), lambda bi, i: (bi, i, 0)),
                   pl.BlockSpec((1, heads, hd, hd), lambda bi, i: (bi, 0, 0, 0))],
        out_shape=[jax.ShapeDtypeStruct((B, L, D), F32), jax.ShapeDtypeStruct((B, heads, hd, hd), F32)],
        scratch_shapes=[pltpu.VMEM((heads, hd, hd), F32),
                        pltpu.VMEM((n_gap, T, D), BF16),
                        pltpu.VMEM((T, D), BF16),
                        pltpu.VMEM((T, D), BF16),
                        pltpu.VMEM((T, D), BF16),
                        pltpu.VMEM((T, D), BF16),
                        pltpu.VMEM((T, D), F32),
                        pltpu.VMEM((T, D), F32)],
        compiler_params=_compiler_params(("parallel", "arbitrary")),
        name="hgrn_layer",
    )(x, w_in, w_o, lb_raw, norm_g, g, b, s0, tri)


def kernel(x_prompt, x_sample, cache_fox_k, cache_fox_v, cache_fox_logf, state_pool, state_hgrn, pool_w, pool_scale, fox_w_in, fox_b_f, fox_w_o, hgrn_w_in, hgrn_lb_raw, hgrn_norm_g, hgrn_w_o, ln_mix_g, ln_mix_b, w_up, w_down, ln_ffn_g, ln_ffn_b):
    depth = w_up.shape[0]
    alpha = float((2 * depth) ** 0.25)
    B, L, D = x_prompt.shape
    Bs, Ls, _ = x_sample.shape
    past = cache_fox_k.shape[2]
    heads = FOX_HEADS
    hd = D // heads
    pool_state = state_pool.shape[2]
    row = lambda a: a.reshape(1, D)

    xp, xs = x_prompt, x_sample
    pool_p, pool_s = [], []
    fk_p, fv_p, fl_p, fk_s, fv_s, fl_s = [], [], [], [], [], []
    hg_p, hg_s = [], []
    for i in range(depth):
        kind, j = i % N_MIXERS, i // N_MIXERS
        g_mix, b_mix = row(ln_mix_g[i]), row(ln_mix_b[i])
        mlp = (w_up[i].astype(BF16), w_down[i].astype(BF16), row(ln_ffn_g[i]), row(ln_ffn_b[i]))
        if kind == 0:
            w = pool_w[j].astype(BF16)
            scale = row(pool_scale[j])
            hist_p = jnp.zeros((B, POOL_HIST, D), F32)
            hist_s = jnp.pad(state_pool[j], ((0, 0), (POOL_HIST - pool_state, 0), (0, 0)))
            pool_p.append(xp[:, L - pool_state:])
            pool_s.append(xs[:, Ls - pool_state:])
            xp = _pool_mlp(xp, hist_p, w, scale, g_mix, b_mix, mlp, alpha=alpha, pos0=0)
            xs = _pool_mlp(xs, hist_s, w, scale, g_mix, b_mix, mlp, alpha=alpha, pos0=past)
        elif kind == 1:
            w_cat, bf_pad = _fox_weights(fox_w_in[j], fox_b_f[j], heads)
            w_o = fox_w_o[j].astype(BF16)
            k, v, lf, c, kn, qs, ks, vb = _fox_proj(xp, w_cat, bf_pad, jnp.zeros((B, 1, LANES), F32), heads=heads)
            o = _fox_attn(qs, ks, vb, c, jnp.max(kn, (1, 2))[:, :heads], q_off=0, head_dim=hd)
            xp = _proj_mlp(o.reshape(B * L, D), w_o, xp.reshape(B * L, D), g_mix, b_mix, mlp,
                           alpha=alpha).reshape(B, L, D)
            fk_p.append(k)
            fv_p.append(v)
            fl_p.append(lf)
            lf_pad = jnp.pad(cache_fox_logf[j], ((0, 0), (0, 0), (0, LANES - heads)))
            ks_c, vb_c, c_c, kn_c, c_end = _fox_cache(cache_fox_k[j].reshape(Bs, past, D),
                                                      cache_fox_v[j].reshape(Bs, past, D), lf_pad, heads=heads)
            k, v, lf, c, kn, qs, ks, vb = _fox_proj(xs, w_cat, bf_pad, c_end, heads=heads)
            tail = (-(past + Ls)) % LANES
            grow = ((0, 0), (0, tail), (0, 0))
            ks_all = jnp.pad(jnp.concatenate([ks_c, ks], 1), grow)
            vb_all = jnp.pad(jnp.concatenate([vb_c, vb], 1), grow)
            c_all = jnp.pad(jnp.concatenate([c_c, c], 1), grow, mode="edge")
            kn_all = jnp.maximum(jnp.max(kn_c, (1, 2)), jnp.max(kn, (1, 2)))[:, :heads]
            o = _fox_attn(qs, ks_all, vb_all, c_all, kn_all, q_off=past, head_dim=hd)
            xs = _proj_mlp(o.reshape(Bs * Ls, D), w_o, xs.reshape(Bs * Ls, D), g_mix, b_mix, mlp,
                           alpha=alpha).reshape(Bs, Ls, D)
            fk_s.append(k)
            fv_s.append(v)
            fl_s.append(lf)
        else:
            w_in = hgrn_w_in[j].astype(BF16)
            w_o = hgrn_w_o[j].astype(BF16)
            ng = row(hgrn_norm_g[j])
            s0 = jnp.zeros((B,) + state_hgrn.shape[2:], F32)
            xp, sp = _hgrn_layer(xp, s0, w_in, w_o, hgrn_lb_raw, ng, g_mix, b_mix, alpha=alpha, layer=i)
            xs, ss = _hgrn_layer(xs, state_hgrn[j], w_in, w_o, hgrn_lb_raw, ng, g_mix, b_mix, alpha=alpha, layer=i)
            hg_p.append(sp)
            hg_s.append(ss)
            xp = _mlp_ln(xp.reshape(B * L, D), mlp, alpha=alpha).reshape(B, L, D)
            xs = _mlp_ln(xs.reshape(Bs * Ls, D), mlp, alpha=alpha).reshape(Bs, Ls, D)
    return (xp, xs, jnp.stack(pool_p), jnp.stack(pool_s),
            jnp.stack(fk_p), jnp.stack(fv_p), jnp.stack(fl_p),
            jnp.stack(fk_s), jnp.stack(fv_s), jnp.stack(fl_s),
            jnp.stack(hg_p), jnp.stack(hg_s))
```

```python
import functools

import jax
import jax.numpy as jnp
import numpy as np
from jax import lax
from jax.experimental import pallas as pl
from jax.experimental.pallas import tpu as pltpu

F32 = jnp.float32
BF16 = jnp.bfloat16

CHUNK = 64
N_MIXERS = 3
POOL_WINDOWS = (2, 4, 8, 16)
POOL_HIST = 16
FOX_HEADS = 16
HGRN_HEAD_DIM = 128
LN_EPS = 1e-5
RMS_EPS = 1e-6

LANES = 128
SUBLANES = 8
VMEM_LIMIT_BYTES = 56 * 1024 * 1024

HEAD_SLOT = LANES
MASK_VALUE = -1e30
LOG2E = float(np.log2(np.e))
EXP_UNDERFLOW = 90.0
NORM_SLACK = 1.02
DECAY_LIMIT = 60.0


def _compiler_params(semantics):
    return pltpu.CompilerParams(dimension_semantics=semantics, vmem_limit_bytes=VMEM_LIMIT_BYTES)


def _resident(shape):
    return pl.BlockSpec(shape, lambda *_: (0,) * len(shape), pipeline_mode=pl.Buffered(1))


def _pick_tile(n, pref):
    t = min(n, pref)
    while n % t:
        t //= 2
    return t


def _layer_norm(z, g, b):
    mu = jnp.mean(z, -1, keepdims=True)
    zc = z - mu
    var = jnp.mean(zc * zc, -1, keepdims=True)
    return zc * lax.rsqrt(var + LN_EPS) * g + b


def _split3(x):
    hi = x.astype(BF16)
    r1 = x - hi.astype(F32)
    mid = r1.astype(BF16)
    lo = (r1 - mid.astype(F32)).astype(BF16)
    return hi, mid, lo


def _dot(a, b):
    return jnp.dot(a, b, preferred_element_type=F32)


def _dot_nt(a, b):
    return lax.dot_general(a, b, (((1,), (1,)), ((), ())), preferred_element_type=F32)


def _dot_tn(a, b):
    return lax.dot_general(a, b, (((0,), (0,)), ((), ())), preferred_element_type=F32)


def _dot_exact_lhs01(m01, x, pieces=3):
    hi, mid, lo = _split3(x)
    y = _dot(m01, hi) + _dot(m01, mid)
    return y + _dot(m01, lo) if pieces == 3 else y


FF_CHUNK = 512


def _mlp_ln_value(x, wup_ref, wdn_ref, g, b, alpha):
    xb = x.astype(BF16)
    d_ff = wup_ref.shape[1]
    ff_chunk = _pick_tile(d_ff, FF_CHUNK)
    acc = jnp.zeros(x.shape, F32)
    for c in range(d_ff // ff_chunk):
        sl = slice(c * ff_chunk, (c + 1) * ff_chunk)
        h = jnp.maximum(_dot(xb, wup_ref[:, sl]), 0.0)
        acc = acc + _dot((h * h).astype(BF16), wdn_ref[sl, :])
    return _layer_norm(alpha * x + acc, g, b)


def _mlp_specs(D, d_ff):
    return [_resident((D, d_ff)), _resident((d_ff, D)), _resident((1, D)), _resident((1, D))]


def _mlp_ln_body(x_ref, wup_ref, wdn_ref, g_ref, b_ref, o_ref, *, alpha):
    o_ref[...] = _mlp_ln_value(x_ref[...], wup_ref, wdn_ref, g_ref[...], b_ref[...], alpha)


def _mlp_ln(x2, mlp, *, alpha):
    R, D = x2.shape
    tm = _pick_tile(R, 512)
    return pl.pallas_call(
        functools.partial(_mlp_ln_body, alpha=alpha),
        grid=(R // tm,),
        in_specs=[pl.BlockSpec((tm, D), lambda i: (i, 0))] + _mlp_specs(D, mlp[0].shape[1]),
        out_specs=pl.BlockSpec((tm, D), lambda i: (i, 0)),
        out_shape=jax.ShapeDtypeStruct((R, D), F32),
        compiler_params=_compiler_params(("parallel",)),
        name="mlp_ln",
    )(x2, *mlp)


def _proj_mlp_body(a_ref, w_ref, x_ref, g_ref, b_ref, wup_ref, wdn_ref, g2_ref, b2_ref, o_ref, *, alpha):
    y = _dot(a_ref[...], w_ref[...])
    x1 = _layer_norm(alpha * x_ref[...] + y, g_ref[...], b_ref[...])
    o_ref[...] = _mlp_ln_value(x1, wup_ref, wdn_ref, g2_ref[...], b2_ref[...], alpha)


def _proj_mlp(a2, w, x2, g, b, mlp, *, alpha):
    R, D = x2.shape
    tm = _pick_tile(R, 512)
    return pl.pallas_call(
        functools.partial(_proj_mlp_body, alpha=alpha),
        grid=(R // tm,),
        in_specs=[pl.BlockSpec((tm, D), lambda i: (i, 0)),
                  _resident((D, D)),
                  pl.BlockSpec((tm, D), lambda i: (i, 0)),
                  _resident((1, D)), _resident((1, D))] + _mlp_specs(D, mlp[0].shape[1]),
        out_specs=pl.BlockSpec((tm, D), lambda i: (i, 0)),
        out_shape=jax.ShapeDtypeStruct((R, D), F32),
        compiler_params=_compiler_params(("parallel",)),
        name="proj_mlp",
    )(a2, w, x2, g, b, *mlp)


def _pool_mlp_body(x_ref, xprev_ref, hist_ref, w_ref, scale_ref, g_ref, b_ref, wup_ref, wdn_ref, g2_ref, b2_ref,
                   o_ref, xa_ref, x1_ref, *, alpha, pos0, tm, n_seq):
    i = pl.program_id(1)
    group = x_ref.shape[2] // len(POOL_WINDOWS)
    n_avail = lax.broadcasted_iota(jnp.int32, (tm, 1), 0) + (i * tm + pos0 + 1)
    for q in range(n_seq):
        x = x_ref[q]
        xa_ref[0:POOL_HIST, :] = jnp.where(i == 0, hist_ref[q], xprev_ref[q])
        xa_ref[POOL_HIST:, :] = x
        ys = []
        for gi, w in enumerate(POOL_WINDOWS):
            sl = slice(gi * group, (gi + 1) * group)
            s = xa_ref[:, sl]
            sh = 1
            while sh < w:
                s = s + pltpu.roll(s, sh, axis=0)
                sh *= 2
            cnt = jnp.minimum(n_avail, w).astype(F32)
            pooled = s[POOL_HIST:, :] / cnt - x[:, sl]
            ys.append(_dot(pooled.astype(BF16), w_ref[gi]))
        y = jnp.concatenate(ys, -1) * scale_ref[...]
        x1_ref[q * tm:(q + 1) * tm, :] = _layer_norm(alpha * x + y, g_ref[...], b_ref[...])
    out = _mlp_ln_value(x1_ref[...], wup_ref, wdn_ref, g2_ref[...], b2_ref[...], alpha)
    for q in range(n_seq):
        o_ref[q] = out[q * tm:(q + 1) * tm, :]


def _pool_mlp(x, hist, w, scale, g, b, mlp, *, alpha, pos0):
    B, L, D = x.shape
    n_g, group = w.shape[0], w.shape[1]
    tm = _pick_tile(L, 512)
    n_seq = _pick_tile(B, 512 // tm) if tm == L else 1
    assert tm % POOL_HIST == 0 and max(POOL_WINDOWS) <= POOL_HIST
    per = tm // POOL_HIST
    return pl.pallas_call(
        functools.partial(_pool_mlp_body, alpha=alpha, pos0=pos0, tm=tm, n_seq=n_seq),
        grid=(B // n_seq, L // tm),
        in_specs=[pl.BlockSpec((n_seq, tm, D), lambda bi, i: (bi, i, 0)),
                  pl.BlockSpec((n_seq, POOL_HIST, D), lambda bi, i: (bi, jnp.maximum(i * per - 1, 0), 0)),
                  pl.BlockSpec((n_seq, POOL_HIST, D), lambda bi, i: (bi, 0, 0)),
                  _resident((n_g, group, group)),
                  _resident((1, D)), _resident((1, D)), _resident((1, D))] + _mlp_specs(D, mlp[0].shape[1]),
        out_specs=pl.BlockSpec((n_seq, tm, D), lambda bi, i: (bi, i, 0)),
        out_shape=jax.ShapeDtypeStruct((B, L, D), F32),
        scratch_shapes=[pltpu.VMEM((tm + POOL_HIST, D), F32), pltpu.VMEM((n_seq * tm, D), F32)],
        compiler_params=_compiler_params(("parallel", "arbitrary")),
        name="pool_mlp",
    )(x, x, hist, w, scale, g, b, *mlp)


def _placement_matrices(heads, head_dim):
    assert 3 * heads < LANES and head_dim + 6 <= HEAD_SLOT
    pq = np.zeros((LANES, heads * HEAD_SLOT), np.float32)
    pk = np.zeros((LANES, heads * HEAD_SLOT), np.float32)
    one = 3 * heads
    for h in range(heads):
        base = h * HEAD_SLOT + head_dim
        for p in range(3):
            pq[p * heads + h, base + p] = 1.0
            pq[one, base + 3 + p] = 1.0
            pk[one, base + p] = 1.0
            pk[p * heads + h, base + 3 + p] = -1.0
    return jnp.asarray(pq, BF16), jnp.asarray(pk, BF16)


def _bias_parts(c, heads):
    hi, mid, lo = _split3(c)
    lane = lax.broadcasted_iota(jnp.int32, c.shape, 1)
    parts = (hi.astype(F32) + pltpu.roll(mid.astype(F32), heads, axis=1)
             + pltpu.roll(lo.astype(F32), 2 * heads, axis=1) + jnp.where(lane == 3 * heads, 1.0, 0.0))
    return parts.astype(BF16)


def _slotted(nat, aug, head_dim):
    T = nat.shape[0]
    per = LANES // head_dim
    assert per == 2
    lane = lax.broadcasted_iota(jnp.int32, (T, LANES), 1)
    low = lane < head_dim
    out = []
    for p in range(nat.shape[1] // LANES):
        pair = nat[:, p * LANES:(p + 1) * LANES]
        swapped = pltpu.roll(pair, head_dim, axis=1)
        out.append(jnp.where(low, pair, aug[:, (2 * p) * HEAD_SLOT:(2 * p + 1) * HEAD_SLOT]))
        out.append(jnp.where(low, swapped, aug[:, (2 * p + 1) * HEAD_SLOT:(2 * p + 2) * HEAD_SLOT]))
    return jnp.concatenate(out, -1).astype(BF16)


def _log_sigmoid(x):
    return jnp.minimum(x, 0.0) - jnp.log(1.0 + jnp.exp(-jnp.abs(x)))


def _key_norm_matrix(heads, head_dim):
    g = np.zeros((heads * head_dim, LANES), np.float32)
    for h in range(heads):
        g[h * head_dim:(h + 1) * head_dim, h] = 1.0
    return jnp.asarray(g, BF16)


def _max_sq_key_norm(k, gn):
    return jnp.max(_dot((k * k).astype(BF16), gn), 0, keepdims=True)


def _fox_proj_body(x_ref, w_ref, bf_ref, c0_ref, tri_ref, pq_ref, pk_ref, gn_ref,
                   k_ref, v_ref, lf_ref, c_ref, kn_ref, qs_ref, ks_ref, vb_ref, carry_ref, *, heads, transposed):
    l = pl.program_id(1)
    D = x_ref.shape[2]

    @pl.when(l == 0)
    def _():
        carry_ref[...] = c0_ref[0]

    h = _dot(x_ref[0].astype(BF16), w_ref[...])
    q, k, v = h[:, :D], h[:, D:2 * D], h[:, 2 * D:3 * D]
    fr = h[:, 3 * D:] + bf_ref[...]
    lane = lax.broadcasted_iota(jnp.int32, fr.shape, 1)
    logf = jnp.where(lane < heads, _log_sigmoid(fr), 0.0)
    c = _dot_exact_lhs01(tri_ref[...], logf) + carry_ref[...]
    carry_ref[...] = c[c.shape[0] - 1:, :]
    parts = _bias_parts(c * LOG2E, heads)
    hd = D // heads
    k_ref[0] = k.T if transposed else k
    v_ref[0] = v.T if transposed else v
    vb_ref[0] = v.astype(BF16)
    lf_ref[0] = logf[:, :heads]
    c_ref[0] = c[:, :heads]
    kn_ref[0, 0] = _max_sq_key_norm(k, gn_ref[...])
    qs_ref[0] = _slotted(q, _dot(parts, pq_ref[...]), hd)
    ks_ref[0] = _slotted(k, _dot(parts, pk_ref[...]), hd)


def _fox_proj(x, w_cat, bf_pad, c0, *, heads):
    B, L, D = x.shape
    T = _pick_tile(L, 256)
    transposed = T % LANES == 0
    kv_spec = pl.BlockSpec((1, D, T), lambda bi, i: (bi, 0, i)) if transposed else None
    kv_shape = jax.ShapeDtypeStruct((B, D, L) if transposed else (B, L, D), F32)
    tri = jnp.asarray(np.tril(np.ones((T, T), np.float32)), BF16)
    pq, pk = _placement_matrices(heads, D // heads)
    gn = _key_norm_matrix(heads, D // heads)
    S = heads * HEAD_SLOT
    row = lambda n: pl.BlockSpec((1, T, n), lambda bi, i: (bi, i, 0))
    k, v, *rest = pl.pallas_call(
        functools.partial(_fox_proj_body, heads=heads, transposed=transposed),
        grid=(B, L // T),
        in_specs=[row(D), _resident(w_cat.shape), _resident((1, LANES)),
                  pl.BlockSpec((1, 1, LANES), lambda bi, i: (bi, 0, 0)),
                  _resident((T, T)), _resident((LANES, S)), _resident((LANES, S)), _resident((D, LANES))],
        out_specs=[kv_spec or row(D), kv_spec or row(D), row(heads), row(heads),
                   pl.BlockSpec((1, 1, 1, LANES), lambda bi, i: (bi, i, 0, 0)),
                   row(S), row(S), row(D)],
        out_shape=[kv_shape, kv_shape,
                   jax.ShapeDtypeStruct((B, L, heads), F32), jax.ShapeDtypeStruct((B, L, heads), F32),
                   jax.ShapeDtypeStruct((B, L // T, 1, LANES), F32),
                   jax.ShapeDtypeStruct((B, L, S), BF16), jax.ShapeDtypeStruct((B, L, S), BF16),
                   jax.ShapeDtypeStruct((B, L, D), BF16)],
        scratch_shapes=[pltpu.VMEM((1, LANES), F32)],
        compiler_params=_compiler_params(("parallel", "arbitrary")),
        name="fox_proj",
    )(x, w_cat, bf_pad, c0, tri, pq, pk, gn)
    hd = D // heads
    if transposed:
        k, v = (jnp.transpose(a.reshape(B, heads, hd, L), (0, 3, 1, 2)) for a in (k, v))
    else:
        k, v = k.reshape(B, L, heads, hd), v.reshape(B, L, heads, hd)
    return (k, v, *rest)


def _fox_cache_body(k_ref, v_ref, lf_ref, tri_ref, pk_ref, gn_ref,
                    ks_ref, vb_ref, c_ref, kn_ref, cend_ref, carry_ref, *, heads):
    l = pl.program_id(1)

    @pl.when(l == 0)
    def _():
        carry_ref[...] = jnp.zeros(carry_ref.shape, F32)

    D = k_ref.shape[1]
    k = k_ref[0].T
    c = _dot_exact_lhs01(tri_ref[...], lf_ref[0]) + carry_ref[...]
    last = c[c.shape[0] - 1:, :]
    carry_ref[...] = last
    cend_ref[0] = last
    c_ref[0] = c[:, :heads]
    kn_ref[0, 0] = _max_sq_key_norm(k, gn_ref[...])
    parts = _bias_parts(c * LOG2E, heads)
    ks_ref[0] = _slotted(k, _dot(parts, pk_ref[...]), D // heads)
    vb_ref[0] = v_ref[0].T.astype(BF16)


def _fox_cache(k, v, lf_pad, *, heads):
    B, D, P = k.shape
    T = _pick_tile(P, 256)
    tri = jnp.asarray(np.tril(np.ones((T, T), np.float32)), BF16)
    _, pk = _placement_matrices(heads, D // heads)
    gn = _key_norm_matrix(heads, D // heads)
    S = heads * HEAD_SLOT
    row = lambda n: pl.BlockSpec((1, T, n), lambda bi, i: (bi, i, 0))
    kv_in = pl.BlockSpec((1, D, T), lambda bi, i: (bi, 0, i))
    return pl.pallas_call(
        functools.partial(_fox_cache_body, heads=heads),
        grid=(B, P // T),
        in_specs=[kv_in, kv_in, row(LANES), _resident((T, T)), _resident((LANES, S)), _resident((D, LANES))],
        out_specs=[row(S), row(D), row(heads),
                   pl.BlockSpec((1, 1, 1, LANES), lambda bi, i: (bi, i, 0, 0)),
                   pl.BlockSpec((1, 1, LANES), lambda bi, i: (bi, 0, 0))],
        out_shape=[jax.ShapeDtypeStruct((B, P, S), BF16), jax.ShapeDtypeStruct((B, P, D), BF16),
                   jax.ShapeDtypeStruct((B, P, heads), F32),
                   jax.ShapeDtypeStruct((B, P // T, 1, LANES), F32),
                   jax.ShapeDtypeStruct((B, 1, LANES), F32)],
        scratch_shapes=[pltpu.VMEM((1, LANES), F32)],
        compiler_params=_compiler_params(("parallel", "arbitrary")),
        name="fox_cache",
    )(k, v, lf_pad, tri, pk, gn)


def _fox_attn_body(q_ref, k_ref, v_ref, cend_ref, kn_ref, o_ref, s_ref, p_ref, m_ref, l_ref, acc_ref,
                   *, tq, tk, q_off, head_dim, strip):
    i = pl.program_id(2)
    q_lo = q_off + i * tq
    j_diag = (q_lo + tq - 1) // tk
    lane = lax.broadcasted_iota(jnp.int32, (tq, LANES), 1)
    blk = lax.broadcasted_iota(jnp.int32, (1, LANES), 1)
    n_head = LANES // head_dim
    qs = [q_ref[0, :, e * HEAD_SLOT:(e + 1) * HEAD_SLOT] for e in range(n_head)]

    def step(j, masked):
        start = pl.multiple_of(j * tk, tk)
        v = v_ref[0, pl.ds(start, tk), :]
        m_news = []
        for e in range(n_head):
            k = k_ref[0, pl.ds(start, tk), e * HEAD_SLOT:(e + 1) * HEAD_SLOT]
            s = _dot_nt(qs[e], k)
            if masked:
                qpos = q_lo + lax.broadcasted_iota(jnp.int32, (tq, 1), 0)
                kpos = start + lax.broadcasted_iota(jnp.int32, (1, tk), 1)
                s = jnp.where(kpos <= qpos, s, MASK_VALUE)
            s_ref[e] = s
            m_news.append(jnp.broadcast_to(jnp.max(s, -1, keepdims=True), (tq, LANES)))
        for e in range(n_head):
            m = m_ref[e]
            m_new = jnp.maximum(m, m_news[e])
            m_ref[e] = m_new
            a = jnp.exp2(m - m_new)
            for r in range(tq // strip):
                rows = slice(r * strip, (r + 1) * strip)
                part = jnp.zeros((strip, LANES), F32)
                for t in range(tk // LANES):
                    cols = slice(t * LANES, (t + 1) * LANES)
                    p = jnp.exp2(s_ref[e, rows, cols] - m_new[rows])
                    p_ref[e, rows, cols] = p.astype(BF16)
                    part = part + p
                l_ref[e, rows, :] = a[rows] * l_ref[e, rows, :] + part
            acc_ref[e] = a * acc_ref[e] + _dot(p_ref[e], v)

    m_ref[...] = jnp.full(m_ref.shape, MASK_VALUE, F32)
    l_ref[...] = jnp.zeros(l_ref.shape, F32)
    acc_ref[...] = jnp.zeros(acc_ref.shape, F32)
    step(j_diag, True)
    n_live = jnp.int32(0)
    for e in range(n_head):
        qf = qs[e].astype(F32)
        q_norm = jnp.sqrt(jnp.sum(jnp.where(lane < head_dim, qf * qf, 0.0), -1, keepdims=True))
        c_q = jnp.sum(jnp.where((lane >= head_dim) & (lane < head_dim + 3), qf, 0.0), -1, keepdims=True)
        k_norm = jnp.sqrt(kn_ref[0, 0, e:e + 1, 0:1]) * NORM_SLACK
        reach = jnp.max(q_norm * k_norm + c_q - m_ref[e, :, 0:1], 0, keepdims=True)
        live = (blk < j_diag) & (cend_ref[0, 0, e:e + 1, :] * LOG2E <= reach + EXP_UNDERFLOW * LOG2E)
        n_live = jnp.maximum(n_live, jnp.sum(jnp.where(live, 1, 0)))

    def body(t, _):
        step(j_diag - 1 - t, False)
        return 0

    lax.fori_loop(0, n_live, body, 0)
    outs = [acc_ref[e] / jnp.sum(l_ref[e], -1, keepdims=True) for e in range(n_head)]
    o_ref[0] = jnp.where(lane < head_dim, outs[0], outs[1]).astype(BF16)


def _fox_attn(qs, ks, vb, c_keys, kn2, *, q_off, head_dim):
    B, Lq, S = qs.shape
    Lk = ks.shape[1]
    D = vb.shape[2]
    heads = D // head_dim
    tq = _pick_tile(Lq, 512)
    tk = Lk if Lk <= 2048 else _pick_tile(Lk, 512)
    n_blk = Lk // tk
    assert tk % tq == 0 and q_off % tq == 0 and q_off + Lq <= Lk and LANES // head_dim == 2 and n_blk <= LANES
    cend = jnp.swapaxes(c_keys[:, tk - 1::tk, :], 1, 2)
    cend = jnp.pad(cend, ((0, 0), (0, 0), (0, LANES - n_blk))).reshape(B, heads // 2, 2, LANES)
    knb = jnp.broadcast_to(kn2.reshape(B, heads // 2, 2, 1), (B, heads // 2, 2, LANES))
    pair = lambda: pl.BlockSpec((1, 1, 2, LANES), lambda bi, p, i: (bi, p, 0, 0))
    return pl.pallas_call(
        functools.partial(_fox_attn_body, tq=tq, tk=tk, q_off=q_off, head_dim=head_dim, strip=min(tq, 32)),
        grid=(B, D // LANES, Lq // tq),
        in_specs=[pl.BlockSpec((1, tq, 2 * HEAD_SLOT), lambda bi, p, i: (bi, i, p)),
                  pl.BlockSpec((1, Lk, 2 * HEAD_SLOT), lambda bi, p, i: (bi, 0, p)),
                  pl.BlockSpec((1, Lk, LANES), lambda bi, p, i: (bi, 0, p)),
                  pair(), pair()],
        out_specs=pl.BlockSpec((1, tq, LANES), lambda bi, p, i: (bi, i, p)),
        out_shape=jax.ShapeDtypeStruct((B, Lq, D), BF16),
        scratch_shapes=[pltpu.VMEM((2, tq, tk), F32),
                        pltpu.VMEM((2, tq, tk), BF16),
                        pltpu.VMEM((2, tq, LANES), F32),
                        pltpu.VMEM((2, tq, LANES), F32),
                        pltpu.VMEM((2, tq, LANES), F32)],
        compiler_params=_compiler_params(("parallel", "parallel", "arbitrary")),
        name="fox_attn",
    )(qs, ks, vb, cend, knb)


def _fox_weights(w_in, b_f, heads):
    D = w_in.shape[0]
    scale = (D // heads) ** -0.5 * LOG2E
    wf = jnp.pad(w_in[:, 3 * D:], ((0, 0), (0, LANES - heads)))
    w_cat = jnp.concatenate([w_in[:, :D] * scale, w_in[:, D:3 * D], wf], axis=1).astype(BF16)
    bf_pad = jnp.pad(b_f, (0, LANES - heads)).reshape(1, LANES)
    return w_cat, bf_pad


def _sigmoid(x):
    return 1.0 / (1.0 + jnp.exp(-x))


def _hgrn_body(x_ref, win_ref, wo_ref, lbraw_ref, ng_ref, g_ref, b_ref, s0_ref, tri_ref,
               y_ref, sfin_ref, st_ref, qg_ref, kt_ref, kh_ref, qi_ref, vb_ref, eb_ref, o_ref,
               *, alpha, layer, T, heads):
    l = pl.program_id(1)
    D = x_ref.shape[2]
    hd = D // heads
    n_sub = CHUNK // SUBLANES
    n_chunk = T // CHUNK

    @pl.when(l == 0)
    def _():
        for h in range(heads):
            st_ref[h] = s0_ref[0, h].T

    lbr = lbraw_ref[...]
    e = jnp.exp(lbr - jnp.max(lbr, 0, keepdims=True))
    pr = e / jnp.sum(e, 0, keepdims=True)
    lb = jnp.zeros((1, D), F32)
    for r in range(1, layer + 1):
        lb = lb + pr[r:r + 1, :]

    x = x_ref[0]
    hp = _dot(x.astype(BF16), win_ref[...])
    qr, fr, v, gate = hp[:, :D], hp[:, D:2 * D], hp[:, 2 * D:3 * D], hp[:, 3 * D:]
    q = qr * _sigmoid(qr)
    logf = jnp.log(lb + (1.0 - lb) * _sigmoid(fr))
    kk = (1.0 - lb) * _sigmoid(-fr)
    b = _dot_exact_lhs01(tri_ref[...], logf, pieces=2)
    bc = b.reshape(n_chunk, CHUNK, D)
    b_last = jnp.broadcast_to(bc[:, CHUNK - 1:, :], bc.shape).reshape(T, D)

    kh_ref[...] = (kk * jnp.exp(b_last - b)).astype(BF16)
    qi_ref[...] = (q * jnp.exp(b)).astype(BF16)
    vb_ref[...] = v.astype(BF16)
    eb_ref[...] = jnp.exp(b_last)
    mild = jnp.min(b) >= -DECAY_LIMIT

    @pl.when(mild)
    def _():
        kt_ref[...] = (kk * jnp.exp(-b)).astype(BF16)
        row = lax.broadcasted_iota(jnp.int32, (T, T), 0)
        col = lax.broadcasted_iota(jnp.int32, (T, T), 1)
        shift = CHUNK.bit_length() - 1
        keep = (lax.shift_right_logical(row, shift) == lax.shift_right_logical(col, shift)) & (col <= row)
        for h in range(heads):
            sl = slice(h * hd, (h + 1) * hd)
            a = jnp.where(keep, _dot_nt(qi_ref[:, sl], kt_ref[:, sl]), 0.0)
            o_h = _dot(a.astype(BF16), vb_ref[:, sl])
            st = st_ref[h]
            inter = []
            for c in range(n_chunk):
                rc = slice(c * CHUNK, (c + 1) * CHUNK)
                inter.append(_dot_nt(qi_ref[rc, sl], st.astype(BF16)))
                st = st * eb_ref[c * CHUNK:c * CHUNK + 1, sl] + _dot_tn(vb_ref[rc, sl], kh_ref[rc, sl])
            st_ref[h] = st
            o_ref[:, sl] = o_h + jnp.concatenate(inter, 0)

    @pl.when(jnp.logical_not(mild))
    def _():
        shape3 = (T // SUBLANES, SUBLANES, D)
        b3 = b.reshape(shape3)
        lf3 = logf.reshape(shape3)
        b_end = jnp.broadcast_to(b3[:, SUBLANES - 1:, :], shape3).reshape(T, D)
        b_start = jnp.broadcast_to(b3[:, :1, :] - lf3[:, :1, :], shape3).reshape(T, D)
        qt = q * jnp.exp(jnp.minimum(b - b_start, 0.0))
        dm = jnp.exp(jnp.minimum(b_end - b_start, 0.0))
        qg_ref[0] = qt.astype(BF16)
        for gp in range(1, n_sub - 1):
            shifted = jnp.concatenate([jnp.ones((gp * SUBLANES, D), F32), dm[:T - gp * SUBLANES, :]], 0)
            qt = qt * shifted
            qg_ref[gp] = qt.astype(BF16)
        kt_ref[...] = (kk * jnp.exp(b_end - b)).astype(BF16)

        q3 = q.reshape(shape3)
        k3 = kk.reshape(shape3)
        v3 = v.reshape(shape3)
        sub = lax.broadcasted_iota(jnp.int32, (T // SUBLANES, SUBLANES, 1), 1)
        od = [jnp.zeros((T // SUBLANES, SUBLANES, hd), F32) for _ in range(heads)]
        for d in range(SUBLANES):
            kr = k3 if d == 0 else pltpu.roll(k3, d, axis=1)
            br = b3 if d == 0 else pltpu.roll(b3, d, axis=1)
            vr = v3 if d == 0 else pltpu.roll(v3, d, axis=1)
            w = q3 * kr * jnp.exp(jnp.minimum(b3 - br, 0.0))
            for h in range(heads):
                sl = slice(h * hd, (h + 1) * hd)
                a = jnp.sum(w[:, :, sl], -1, keepdims=True)
                od[h] = od[h] + jnp.where(sub >= d, a, 0.0) * vr[:, :, sl]
        for h in range(heads):
            o_ref[:, h * hd:(h + 1) * hd] = od[h].reshape(T, hd)

        rsub = lax.shift_right_logical(lax.broadcasted_iota(jnp.int32, (CHUNK, CHUNK), 0), 3)
        csub = lax.shift_right_logical(lax.broadcasted_iota(jnp.int32, (CHUNK, CHUNK), 1), 3)
        gap = rsub - csub

        def chunk_step(c, _):
            rows = pl.ds(pl.multiple_of(c * CHUNK, CHUNK), CHUNK)
            for h in range(heads):
                sl = slice(h * hd, (h + 1) * hd)
                kt = kt_ref[rows, sl]
                vb = vb_ref[rows, sl]
                a_off = jnp.zeros((CHUNK, CHUNK), F32)
                for gp in range(n_sub - 1):
                    a_g = _dot_nt(qg_ref[gp, rows, sl], kt)
                    a_off = a_off + jnp.where(gap == gp + 1, a_g, 0.0)
                st = st_ref[h]
                o_h = _dot(a_off.astype(BF16), vb) + _dot_nt(qi_ref[rows, sl], st.astype(BF16))
                o_ref[rows, sl] = o_ref[rows, sl] + o_h
                u_t = _dot_tn(vb, kh_ref[rows, sl])
                st_ref[h] = st * eb_ref[pl.ds(c * CHUNK, 1), sl] + u_t
            return 0

        lax.fori_loop(0, n_chunk, chunk_step, 0)

    @pl.when(l == pl.num_programs(1) - 1)
    def _():
        for h in range(heads):
            sfin_ref[0, h] = st_ref[h].T

    outs = []
    for h in range(heads):
        oh = o_ref[:, h * hd:(h + 1) * hd]
        outs.append(oh * lax.rsqrt(jnp.mean(oh * oh, -1, keepdims=True) + RMS_EPS))
    o = jnp.concatenate(outs, -1) * ng_ref[...] * (gate * _sigmoid(gate))
    y = _dot(o.astype(BF16), wo_ref[...])
    y_ref[0] = _layer_norm(alpha * x + y, g_ref[...], b_ref[...])


def _hgrn_layer(x, s0, w_in, w_o, lb_raw, norm_g, g, b, *, alpha, layer):
    B, L, D = x.shape
    heads = D // HGRN_HEAD_DIM
    hd = HGRN_HEAD_DIM
    T = _pick_tile(L, 256)
    assert T % CHUNK == 0
    tri = jnp.asarray(np.kron(np.eye(T // CHUNK, dtype=np.float32), np.tril(np.ones((CHUNK, CHUNK), np.float32))), BF16)
    n_gap = CHUNK // SUBLANES - 1
    return pl.pallas_call(
        functools.partial(_hgrn_body, alpha=alpha, layer=layer, T=T, heads=heads),
        grid=(B, L // T),
        in_specs=[pl.BlockSpec((1, T, D), lambda bi, i: (bi, i, 0)),
                  _resident(w_in.shape), _resident(w_o.shape), _resident(lb_raw.shape),
                  _resident((1, D)), _resident((1, D)), _resident((1, D)),
                  pl.BlockSpec((1, heads, hd, hd), lambda bi, i: (bi, 0, 0, 0)),
                  _resident((T, T))],
        out_specs=[pl.BlockSpec((1, T, D), lambda bi, i: (bi, i, 0)),
                   pl.BlockSpec((1, heads, hd, hd), lambda bi, i: (bi, 0, 0, 0))],
        out_shape=[jax.ShapeDtypeStruct((B, L, D), F32), jax.ShapeDtypeStruct((B, heads, hd, hd), F32)],
        scratch_shapes=[pltpu.VMEM((heads, hd, hd), F32),
                        pltpu.VMEM((n_gap, T, D), BF16),
                        pltpu.VMEM((T, D), BF16),
                        pltpu.VMEM((T, D), BF16),
                        pltpu.VMEM((T, D), BF16),
                        pltpu.VMEM((T, D), BF16),
                        pltpu.VMEM((T, D), F32),
                        pltpu.VMEM((T, D), F32)],
        compiler_params=_compiler_params(("parallel", "arbitrary")),
        name="hgrn_layer",
    )(x, w_in, w_o, lb_raw, norm_g, g, b, s0, tri)


def kernel(x_prompt, x_sample, cache_fox_k, cache_fox_v, cache_fox_logf, state_pool, state_hgrn, pool_w, pool_scale, fox_w_in, fox_b_f, fox_w_o, hgrn_w_in, hgrn_lb_raw, hgrn_norm_g, hgrn_w_o, ln_mix_g, ln_mix_b, w_up, w_down, ln_ffn_g, ln_ffn_b):
    depth = w_up.shape[0]
    alpha = float((2 * depth) ** 0.25)
    B, L, D = x_prompt.shape
    Bs, Ls, _ = x_sample.shape
    past = cache_fox_k.shape[2]
    heads = FOX_HEADS
    hd = D // heads
    pool_state = state_pool.shape[2]
    row = lambda a: a.reshape(1, D)

    xp, xs = x_prompt, x_sample
    pool_p, pool_s = [], []
    fk_p, fv_p, fl_p, fk_s, fv_s, fl_s = [], [], [], [], [], []
    hg_p, hg_s = [], []
    for i in range(depth):
        kind, j = i % N_MIXERS, i // N_MIXERS
        g_mix, b_mix = row(ln_mix_g[i]), row(ln_mix_b[i])
        mlp = (w_up[i].astype(BF16), w_down[i].astype(BF16), row(ln_ffn_g[i]), row(ln_ffn_b[i]))
        if kind == 0:
            w = pool_w[j].astype(BF16)
            scale = row(pool_scale[j])
            hist_p = jnp.zeros((B, POOL_HIST, D), F32)
            hist_s = jnp.pad(state_pool[j], ((0, 0), (POOL_HIST - pool_state, 0), (0, 0)))
            pool_p.append(xp[:, L - pool_state:])
            pool_s.append(xs[:, Ls - pool_state:])
            xp = _pool_mlp(xp, hist_p, w, scale, g_mix, b_mix, mlp, alpha=alpha, pos0=0)
            xs = _pool_mlp(xs, hist_s, w, scale, g_mix, b_mix, mlp, alpha=alpha, pos0=past)
        elif kind == 1:
            w_cat, bf_pad = _fox_weights(fox_w_in[j], fox_b_f[j], heads)
            w_o = fox_w_o[j].astype(BF16)
            k, v, lf, c, kn, qs, ks, vb = _fox_proj(xp, w_cat, bf_pad, jnp.zeros((B, 1, LANES), F32), heads=heads)
            o = _fox_attn(qs, ks, vb, c, jnp.max(kn, (1, 2))[:, :heads], q_off=0, head_dim=hd)
            xp = _proj_mlp(o.reshape(B * L, D), w_o, xp.reshape(B * L, D), g_mix, b_mix, mlp,
                           alpha=alpha).reshape(B, L, D)
            fk_p.append(k)
            fv_p.append(v)
            fl_p.append(lf)
            lf_pad = jnp.pad(cache_fox_logf[j], ((0, 0), (0, 0), (0, LANES - heads)))
            kt_c, vt_c = (jnp.transpose(a[j], (0, 2, 3, 1)).reshape(Bs, D, past) for a in (cache_fox_k, cache_fox_v))
            ks_c, vb_c, c_c, kn_c, c_end = _fox_cache(kt_c, vt_c, lf_pad, heads=heads)
            k, v, lf, c, kn, qs, ks, vb = _fox_proj(xs, w_cat, bf_pad, c_end, heads=heads)
            tail = (-(past + Ls)) % LANES
            grow = ((0, 0), (0, tail), (0, 0))
            ks_all = jnp.pad(jnp.concatenate([ks_c, ks], 1), grow)
            vb_all = jnp.pad(jnp.concatenate([vb_c, vb], 1), grow)
            c_all = jnp.pad(jnp.concatenate([c_c, c], 1), grow, mode="edge")
            kn_all = jnp.maximum(jnp.max(kn_c, (1, 2)), jnp.max(kn, (1, 2)))[:, :heads]
            o = _fox_attn(qs, ks_all, vb_all, c_all, kn_all, q_off=past, head_dim=hd)
            xs = _proj_mlp(o.reshape(Bs * Ls, D), w_o, xs.reshape(Bs * Ls, D), g_mix, b_mix, mlp,
                           alpha=alpha).reshape(Bs, Ls, D)
            fk_s.append(k)
            fv_s.append(v)
            fl_s.append(lf)
        else:
            w_in = hgrn_w_in[j].astype(BF16)
            w_o = hgrn_w_o[j].astype(BF16)
            ng = row(hgrn_norm_g[j])
            s0 = jnp.zeros((B,) + state_hgrn.shape[2:], F32)
            xp, sp = _hgrn_layer(xp, s0, w_in, w_o, hgrn_lb_raw, ng, g_mix, b_mix, alpha=alpha, layer=i)
            xs, ss = _hgrn_layer(xs, state_hgrn[j], w_in, w_o, hgrn_lb_raw, ng, g_mix, b_mix, alpha=alpha, layer=i)
            hg_p.append(sp)
            hg_s.append(ss)
            xp = _mlp_ln(xp.reshape(B * L, D), mlp, alpha=alpha).reshape(B, L, D)
            xs = _mlp_ln(xs.reshape(Bs * Ls, D), mlp, alpha=alpha).reshape(Bs, Ls, D)
    return (xp, xs, jnp.stack(pool_p), jnp.stack(pool_s),
            jnp.stack(fk_p), jnp.stack(fv_p), jnp.stack(fl_p),
            jnp.stack(fk_s), jnp.stack(fv_s), jnp.stack(fl_s),
            jnp.stack(hg_p), jnp.stack(hg_s))
```
